```python
import jax, jax.numpy as jnp
from jax import lax
import numpy as np

D_MODEL = 4096
BATCH = 4
SEQ = 2048
DEPTH = 4
DEC_BATCH = 8
DEC_SEQ = 4
PAST_LEN = 8192
PAGE_SIZE = 128

N_HEADS = 32
HEAD_DIM = D_MODEL // N_HEADS
SB_BLOCK = 128
SB_BIAS_INIT = -10.0
CHUNK = 128
CM_DIM = D_MODEL
CM_GROUPS = 16
CM_GROUP_DIM = CM_DIM // CM_GROUPS
D_FF = 4 * D_MODEL
PLE_DIM = 256
N_SB = (DEPTH + 1) // 2
N_CM = DEPTH // 2
EPS = 1e-6

kernel_name = 'stick_breaking_gmlp_hybrid_step'


def rms_norm(x, g):
    xf = x.astype(jnp.float32)
    y = xf * lax.rsqrt(jnp.mean(xf * xf, axis=-1, keepdims=True) + EPS)
    return (y * g.astype(jnp.float32)).astype(x.dtype)


def split_heads(qkv):
    b, n, _ = qkv.shape
    q, k, v = jnp.split(qkv, 3, axis=-1)
    return (q.reshape(b, n, N_HEADS, HEAD_DIM), k.reshape(b, n, N_HEADS, HEAD_DIM),
            v.reshape(b, n, N_HEADS, HEAD_DIM))


def _sb_block(q, k, v, bias, q_pos, k_pos):
    z = (jnp.einsum('bqhd,bkhd->bhqk', q, k).astype(jnp.float32) * (HEAD_DIM ** -0.5)
         + bias.astype(jnp.float32)[None, :, None, None])
    causal = k_pos[None, :] < q_pos[:, None]
    log_keep = jnp.where(causal, jax.nn.log_sigmoid(-z), 0.0)
    log_after = lax.cumsum(log_keep, axis=3, reverse=True) - log_keep
    w = jnp.where(causal, jnp.exp(jax.nn.log_sigmoid(z) + log_after), 0.0)
    return jnp.einsum('bhqk,bkhd->bqhd', w.astype(v.dtype), v)


def stick_breaking_attention(q, k, v, bias, q_offset):
    n = q.shape[1]
    L = k.shape[1]
    blk = SB_BLOCK if n % SB_BLOCK == 0 else n
    outs = []
    for i in range(n // blk):
        lo = i * blk
        k_hi = min(L, q_offset + lo + blk)
        q_pos = q_offset + lo + jnp.arange(blk)
        k_pos = jnp.arange(k_hi)
        outs.append(_sb_block(q[:, lo:lo + blk], k[:, :k_hi], v[:, :k_hi], bias, q_pos, k_pos))
    return jnp.concatenate(outs, axis=1)


def chunk_token_mix(v, w_s, b_s):
    b, n, dv = v.shape
    nc = -(-n // CHUNK)
    vp = jnp.pad(v, ((0, 0), (0, nc * CHUNK - n), (0, 0)))
    vp = vp.reshape(b, nc, CHUNK, CM_GROUPS, CM_GROUP_DIM)
    w = w_s * jnp.tril(jnp.ones((CHUNK, CHUNK), w_s.dtype))
    mixed = jnp.einsum('gts,bcsgd->bctgd', w, vp) + b_s.T[None, None, :, :, None]
    return mixed.reshape(b, nc * CHUNK, dv)[:, :n]


def gmlp_mixer(h, w_uv, g_v, w_s, b_s, w_o):
    z = jax.nn.gelu(h @ w_uv)
    u, v = jnp.split(z, 2, axis=-1)
    v = rms_norm(v, g_v)
    out = (u * chunk_token_mix(v, w_s, b_s)) @ w_o
    return out, v


def channel_and_ple(x, p, g_pre, g_post, w_up, w_down, w_pl_proj, g_pl, w_pl_gate):
    h = rms_norm(x, g_pre)
    f = jnp.square(jax.nn.relu(h @ w_up)) @ w_down
    x = x + rms_norm(f, g_post)
    gate = jax.nn.sigmoid(rms_norm(x, g_pl) @ w_pl_gate)
    return x + gate * (p @ w_pl_proj)


def setup_inputs(seed: int = 0) -> dict:
    key = jax.random.key(seed)
    ks = jax.random.split(key, 24)
    n_pages = PAST_LEN // PAGE_SIZE
    n_used = DEC_BATCH * n_pages
    n_pool = n_used + n_used // 4

    def nrm(k, shape, scale):
        return scale * jax.random.normal(k, shape, jnp.float32)

    def gain(k, shape):
        return 1.0 + 0.05 * jax.random.normal(k, shape, jnp.float32)

    page_table = jax.random.permutation(ks[4], n_pool)[:n_used].reshape(DEC_BATCH, n_pages).astype(jnp.int32)
    return {
        'x_prompt': nrm(ks[0], (BATCH, SEQ, D_MODEL), 1.0),
        'x_sample': nrm(ks[1], (DEC_BATCH, DEC_SEQ, D_MODEL), 1.0),
        'cache_k': nrm(ks[2], (n_pool, PAGE_SIZE, N_SB, N_HEADS, HEAD_DIM), 1.0),
        'cache_v': nrm(ks[3], (n_pool, PAGE_SIZE, N_SB, N_HEADS, HEAD_DIM), 1.0),
        'page_table': page_table,
        'p_prompt': nrm(ks[5], (DEPTH, BATCH, SEQ, PLE_DIM), 1.0),
        'p_sample': nrm(ks[6], (DEPTH, DEC_BATCH, DEC_SEQ, PLE_DIM), 1.0),
        'g_mix_pre': gain(ks[7], (DEPTH, D_MODEL)),
        'g_mix_post': gain(ks[8], (DEPTH, D_MODEL)),
        'g_ffn_pre': gain(ks[9], (DEPTH, D_MODEL)),
        'g_ffn_post': gain(ks[10], (DEPTH, D_MODEL)),
        'w_qkv': nrm(ks[11], (N_SB, D_MODEL, 3 * D_MODEL), D_MODEL ** -0.5),
        'w_o_sb': nrm(ks[12], (N_SB, D_MODEL, D_MODEL), D_MODEL ** -0.5),
        'b_sb': SB_BIAS_INIT + nrm(ks[23], (N_SB, N_HEADS), 0.5),
        'w_uv': nrm(ks[13], (N_CM, D_MODEL, 2 * CM_DIM), D_MODEL ** -0.5),
        'g_v': gain(ks[14], (N_CM, CM_DIM)),
        'w_s': nrm(ks[15], (N_CM, CM_GROUPS, CHUNK, CHUNK), CHUNK ** -0.5),
        'b_s': 1.0 + nrm(ks[16], (N_CM, CM_GROUPS, CHUNK), 0.1),
        'w_o_cm': nrm(ks[17], (N_CM, CM_DIM, D_MODEL), CM_DIM ** -0.5),
        'w_up': nrm(ks[18], (DEPTH, D_MODEL, D_FF), D_MODEL ** -0.5),
        'w_down': nrm(ks[19], (DEPTH, D_FF, D_MODEL), D_FF ** -0.5),
        'w_pl_proj': nrm(ks[20], (DEPTH, PLE_DIM, D_MODEL), PLE_DIM ** -0.5),
        'g_pl': gain(ks[21], (DEPTH, D_MODEL)),
        'w_pl_gate': nrm(ks[22], (DEPTH, D_MODEL, D_MODEL), D_MODEL ** -0.5),
    }


def reference(x_prompt, x_sample, cache_k, cache_v, page_table, p_prompt, p_sample,
              g_mix_pre, g_mix_post, g_ffn_pre, g_ffn_post, w_qkv, w_o_sb, b_sb,
              w_uv, g_v, w_s, b_s, w_o_cm, w_up, w_down, w_pl_proj, g_pl, w_pl_gate):
    n_pages = PAST_LEN // PAGE_SIZE
    bp, sp, _ = x_prompt.shape
    bs, ss, _ = x_sample.shape
    xp, xs = x_prompt, x_sample
    k_p_rows, v_p_rows, k_s_rows, v_s_rows, cm_s_rows = [], [], [], [], []
    for i in range(DEPTH):
        j = i // 2
        hp = rms_norm(xp, g_mix_pre[i])
        hs = rms_norm(xs, g_mix_pre[i])
        if i % 2 == 0:
            qp, kp, vp = split_heads(hp @ w_qkv[j])
            qs, ks_, vs = split_heads(hs @ w_qkv[j])
            past_k = cache_k[page_table, :, j].reshape(bs, n_pages * PAGE_SIZE, N_HEADS, HEAD_DIM)
            past_v = cache_v[page_table, :, j].reshape(bs, n_pages * PAGE_SIZE, N_HEADS, HEAD_DIM)
            op = stick_breaking_attention(qp, kp, vp, b_sb[j], 0)
            os_ = stick_breaking_attention(qs, jnp.concatenate([past_k, ks_], axis=1),
                                           jnp.concatenate([past_v, vs], axis=1), b_sb[j], PAST_LEN)
            mp = op.reshape(bp, sp, D_MODEL) @ w_o_sb[j]
            ms = os_.reshape(bs, ss, D_MODEL) @ w_o_sb[j]
            k_p_rows.append(kp)
            v_p_rows.append(vp)
            k_s_rows.append(ks_)
            v_s_rows.append(vs)
        else:
            mp, _ = gmlp_mixer(hp, w_uv[j], g_v[j], w_s[j], b_s[j], w_o_cm[j])
            ms, v_rows = gmlp_mixer(hs, w_uv[j], g_v[j], w_s[j], b_s[j], w_o_cm[j])
            cm_s_rows.append(v_rows)
        xp = xp + rms_norm(mp, g_mix_post[i])
        xs = xs + rms_norm(ms, g_mix_post[i])
        xp = channel_and_ple(xp, p_prompt[i], g_ffn_pre[i], g_ffn_post[i], w_up[i], w_down[i],
                             w_pl_proj[i], g_pl[i], w_pl_gate[i])
        xs = channel_and_ple(xs, p_sample[i], g_ffn_pre[i], g_ffn_post[i], w_up[i], w_down[i],
                             w_pl_proj[i], g_pl[i], w_pl_gate[i])
    k_prompt_new = jnp.stack(k_p_rows, axis=2)
    v_prompt_new = jnp.stack(v_p_rows, axis=2)
    k_sample_new = jnp.stack(k_s_rows, axis=2)
    v_sample_new = jnp.stack(v_s_rows, axis=2)
    state_v_sample_new = jnp.stack(cm_s_rows, axis=2)
    return (xp, xs, k_prompt_new, v_prompt_new, k_sample_new, v_sample_new, state_v_sample_new)
```

```python
import functools

import jax
import jax.numpy as jnp
from jax import lax
from jax.experimental import pallas as pl
from jax.experimental.pallas import tpu as pltpu

D_MODEL = 4096
DEPTH = 4
N_HEADS = 32
HEAD_DIM = D_MODEL // N_HEADS
PAGE_SIZE = 128
CHUNK = 128
CM_GROUPS = 16
CM_GROUP_DIM = D_MODEL // CM_GROUPS
PLE_DIM = 256
EPS = 1e-6

V7X_VMEM_LIMIT_BYTES = 56 * 1024 * 1024
HEADS_PER_TILE = 8

F32 = jnp.float32
BF16 = jnp.bfloat16


def _params(*sem):
    return pltpu.CompilerParams(dimension_semantics=sem, vmem_limit_bytes=V7X_VMEM_LIMIT_BYTES)


def _rms(x, g):
    ms = jnp.mean(x * x, axis=-1, keepdims=True)
    return x * lax.rsqrt(ms + EPS) * g


def _norm_kernel(x_ref, g_ref, h_ref):
    h_ref[...] = _rms(x_ref[...], g_ref[...]).astype(BF16)


def _resnorm_kernel(x_ref, m_ref, gp_ref, gn_ref, x1_ref, h_ref):
    x1 = x_ref[...] + _rms(m_ref[...], gp_ref[...])
    x1_ref[...] = x1
    h_ref[...] = _rms(x1, gn_ref[...]).astype(BF16)


def _addnorm_kernel(x_ref, a_ref, gn_ref, x1_ref, h_ref):
    x1 = x_ref[...] + a_ref[...]
    x1_ref[...] = x1
    h_ref[...] = _rms(x1, gn_ref[...]).astype(BF16)


def _add_kernel(x_ref, a_ref, x1_ref):
    x1_ref[...] = x_ref[...] + a_ref[...]


def _row_call(body, arrays, gains, out_dtypes):
    m = arrays[0].shape[0]
    tr = min(m, 256)
    row = pl.BlockSpec((tr, D_MODEL), lambda r: (r, 0))
    in_specs = [row] * len(arrays)
    args = list(arrays)
    for table, layer in gains:
        in_specs.append(pl.BlockSpec((None, 1, D_MODEL), lambda r, layer=layer: (layer, 0, 0)))
        args.append(table.reshape(table.shape[0], 1, D_MODEL))
    outs = pl.pallas_call(
        body,
        grid=(m // tr,),
        in_specs=in_specs,
        out_specs=[row] * len(out_dtypes),
        out_shape=[jax.ShapeDtypeStruct((m, D_MODEL), dt) for dt in out_dtypes],
        compiler_params=_params("arbitrary"),
    )(*args)
    return outs


def _gelu_tanh(x):
    c = 0.7978845608028654
    return 0.5 * x * (1.0 + jnp.tanh(c * (x + 0.044715 * (x * x * x))))


def _mm_kernel(*refs, epi, nk):
    if epi == "ple":
        a_ref, w_ref, p_ref, wp_ref, o_ref = refs
        acc_ref = o_ref
    elif epi == "f32":
        a_ref, w_ref, o_ref = refs
        acc_ref = o_ref
    elif epi == "dual":
        a_ref, w_ref, o_ref, ob_ref = refs
        acc_ref = o_ref
    else:
        a_ref, w_ref, o_ref, acc_ref = refs
    k = pl.program_id(2)

    @pl.when(k == 0)
    def _():
        acc_ref[...] = jnp.zeros_like(acc_ref)

    acc_ref[...] += jnp.dot(a_ref[...], w_ref[...].astype(BF16), preferred_element_type=F32)

    @pl.when(k == nk - 1)
    def _():
        if epi == "dual":
            ob_ref[...] = acc_ref[...].astype(BF16)
        elif epi == "relu2":
            r = jnp.maximum(acc_ref[...], 0.0)
            o_ref[...] = (r * r).astype(o_ref.dtype)
        elif epi == "gelu":
            o_ref[...] = _gelu_tanh(acc_ref[...]).astype(o_ref.dtype)
        elif epi == "ple":
            pp = jnp.dot(p_ref[...].astype(BF16), wp_ref[...].astype(BF16), preferred_element_type=F32)
            o_ref[...] = jax.nn.sigmoid(acc_ref[...]) * pp


def _mm_tiles(m, n, k):
    tm = min(m, 2048)
    tn = min(n, 1024)
    tk = min(k, 1024 if m > 256 else 2048)
    return tm, tn, tk


def _mm(a, w_stack, layer, epi, out_dtype=F32, p=None, wp_stack=None):
    m, kdim = a.shape
    n = w_stack.shape[2]
    tm, tn, tk = _mm_tiles(m, n, kdim)
    nk = kdim // tk
    grid = (m // tm, n // tn, nk)
    in_specs = [
        pl.BlockSpec((tm, tk), lambda i, j, k: (i, k)),
        pl.BlockSpec((None, tk, tn), lambda i, j, k: (layer, k, j)),
    ]
    args = [a, w_stack]
    out_block = pl.BlockSpec((tm, tn), lambda i, j, k: (i, j))
    scratch = []
    if epi == "ple":
        in_specs += [
            pl.BlockSpec((tm, PLE_DIM), lambda i, j, k: (i, 0)),
            pl.BlockSpec((None, PLE_DIM, tn), lambda i, j, k: (layer, 0, j)),
        ]
        args += [p, wp_stack]
    if epi == "dual":
        out_specs = [out_block, out_block]
        out_shape = [jax.ShapeDtypeStruct((m, n), F32), jax.ShapeDtypeStruct((m, n), BF16)]
    else:
        out_specs = out_block
        out_shape = jax.ShapeDtypeStruct((m, n), out_dtype)
        if epi not in ("f32", "ple"):
            scratch = [pltpu.VMEM((tm, tn), F32)]
    return pl.pallas_call(
        functools.partial(_mm_kernel, epi=epi, nk=nk),
        grid=grid,
        in_specs=in_specs,
        out_specs=out_specs,
        out_shape=out_shape,
        scratch_shapes=scratch,
        compiler_params=_params("arbitrary", "arbitrary", "arbitrary"),
    )(*args)


def _neg_softplus(z):
    return -(jnp.maximum(z, 0.0) + jnp.log1p(jnp.exp(-jnp.abs(z))))


def _suffix_sum_lanes(lk, tri):
    hi = lk.astype(BF16)
    lo = (lk - hi.astype(F32)).astype(BF16)
    return (jnp.dot(hi, tri, preferred_element_type=F32)
            + jnp.dot(lo, tri, preferred_element_type=F32))


def _sb_tile(q, kt, vt, bias, tri, carry, acc, valid):
    s = lax.dot_general(q, kt, (((1,), (1,)), ((), ())), preferred_element_type=F32)
    z = s * (HEAD_DIM ** -0.5) + bias
    lk = _neg_softplus(z)
    if valid is not None:
        lk = jnp.where(valid, lk, 0.0)
    la = _suffix_sum_lanes(lk, tri) + carry
    w = jnp.exp(z + lk + la)
    if valid is not None:
        w = jnp.where(valid, w, 0.0)
    acc = acc + jnp.dot(w.astype(BF16), vt, preferred_element_type=F32)
    carry = la[:, 0:1] + lk[:, 0:1]
    return carry, acc


def _strict_upper(n):
    r = lax.broadcasted_iota(jnp.int32, (n, n), 0)
    c = lax.broadcasted_iota(jnp.int32, (n, n), 1)
    return (r > c).astype(BF16)


SB_TILE = 256


def _sbp_kernel(bias_ref, q_ref, k_ref, v_ref, o_ref):
    h = pl.program_id(1)
    qi = pl.program_id(2)
    t = SB_TILE
    q = q_ref[...]
    bias = bias_ref[h]
    tri = _strict_upper(t)
    r = lax.broadcasted_iota(jnp.int32, (t, t), 0)
    c = lax.broadcasted_iota(jnp.int32, (t, t), 1)
    causal = c < r

    def tile(kj, carry, acc, valid):
        start = pl.multiple_of(kj * t, t)
        kt = k_ref[pl.ds(start, t), :]
        vt = v_ref[pl.ds(start, t), :]
        return _sb_tile(q, kt, vt, bias, tri, carry, acc, valid)

    carry = jnp.zeros((t, 1), F32)
    acc = jnp.zeros((t, HEAD_DIM), F32)
    carry, acc = tile(qi, carry, acc, causal)

    def body(i, ca):
        return tile(qi - 1 - i, ca[0], ca[1], None)

    carry, acc = lax.fori_loop(0, qi, body, (carry, acc))
    o_ref[...] = acc.astype(o_ref.dtype)


def _sb_prompt(qkv, bias, batch, seq):
    t = SB_TILE
    nq = seq // t
    return pl.pallas_call(
        _sbp_kernel,
        grid=(batch, N_HEADS, nq),
        in_specs=[
            pl.BlockSpec(memory_space=pltpu.SMEM),
            pl.BlockSpec((t, HEAD_DIM), lambda b, h, qi: (b * nq + qi, h)),
            pl.BlockSpec((seq, HEAD_DIM), lambda b, h, qi: (b, N_HEADS + h)),
            pl.BlockSpec((seq, HEAD_DIM), lambda b, h, qi: (b, 2 * N_HEADS + h)),
        ],
        out_specs=pl.BlockSpec((t, HEAD_DIM), lambda b, h, qi: (b * nq + qi, h)),
        out_shape=jax.ShapeDtypeStruct((batch * seq, D_MODEL), BF16),
        compiler_params=_params("arbitrary", "arbitrary", "arbitrary"),
    )(bias, qkv, qkv, qkv)


def _sbs_kernel(pt_ref, q_ref, kn_ref, vn_ref, k0, k1, k2, k3, v0, v1, v2, v3, bias_ref,
                o_ref, qbd_ref, acc_ref, car_ref, *, n_steps, n_tok):
    del pt_ref
    p = pl.program_id(1)
    rows = n_tok * N_HEADS
    tri = _strict_upper(PAGE_SIZE)
    bias = bias_ref[...]

    def process(kcat, vcat, valid):
        carry, acc = _sb_tile(qbd_ref[...], kcat, vcat, bias, tri, car_ref[...], acc_ref[...], valid)
        car_ref[...] = carry
        acc_ref[...] = acc

    @pl.when(p == 0)
    def _():
        lane_head = lax.broadcasted_iota(jnp.int32, (N_HEADS, D_MODEL), 1) // HEAD_DIM
        row_head = lax.broadcasted_iota(jnp.int32, (N_HEADS, D_MODEL), 0)
        own = lane_head == row_head
        for t in range(n_tok):
            qrow = q_ref[t:t + 1, :]
            qbd_ref[t * N_HEADS:(t + 1) * N_HEADS, :] = jnp.where(own, qrow, 0.0).astype(BF16)
        acc_ref[...] = jnp.zeros_like(acc_ref)
        car_ref[...] = jnp.zeros_like(car_ref)
        tok = lax.broadcasted_iota(jnp.int32, (rows, PAGE_SIZE), 0) // N_HEADS
        key = lax.broadcasted_iota(jnp.int32, (rows, PAGE_SIZE), 1)
        process(kn_ref[...], vn_ref[...], key < tok)

    @pl.when(p > 0)
    def _():
        def cat(parts):
            tiles = []
            for ref in parts:
                flat = ref.reshape(PAGE_SIZE * HEADS_PER_TILE, HEAD_DIM)
                for hh in range(HEADS_PER_TILE):
                    tiles.append(flat[pl.ds(hh, PAGE_SIZE, stride=HEADS_PER_TILE), :])
            return jnp.concatenate(tiles, axis=1).astype(BF16)

        process(cat((k0, k1, k2, k3)), cat((v0, v1, v2, v3)), None)

    @pl.when(p == n_steps - 1)
    def _():
        row_head = lax.broadcasted_iota(jnp.int32, (rows, HEAD_DIM), 0) % N_HEADS
        out = jnp.zeros((rows, HEAD_DIM), F32)
        for h in range(N_HEADS):
            out = out + jnp.where(row_head == h, acc_ref[:, h * HEAD_DIM:(h + 1) * HEAD_DIM], 0.0)
        o_ref[...] = out


def _sb_sample(q, k_new, v_new, cache_k, cache_v, page_table, bias_col, layer):
    bsz, n_tok, _ = q.shape
    n_pages = page_table.shape[1]
    n_pool = cache_k.shape[0]
    n_layers = cache_k.shape[2]
    groups = N_HEADS // HEADS_PER_TILE
    rows = n_tok * N_HEADS
    n_steps = n_pages + 1
    ck = cache_k.reshape(n_pool, PAGE_SIZE, n_layers, groups, HEADS_PER_TILE, HEAD_DIM)
    cv = cache_v.reshape(n_pool, PAGE_SIZE, n_layers, groups, HEADS_PER_TILE, HEAD_DIM)

    def page_spec(g):
        def index(b, p, pt):
            page = pt[b, n_pages - 1 - jnp.maximum(p - 1, 0)]
            return (page, 0, layer, g, 0, 0)
        return pl.BlockSpec((None, PAGE_SIZE, None, None, HEADS_PER_TILE, HEAD_DIM), index)

    seq_spec = pl.BlockSpec((None, PAGE_SIZE, D_MODEL), lambda b, p, pt: (b, 0, 0))
    grid_spec = pltpu.PrefetchScalarGridSpec(
        num_scalar_prefetch=1,
        grid=(bsz, n_steps),
        in_specs=[pl.BlockSpec((None, n_tok, D_MODEL), lambda b, p, pt: (b, 0, 0)), seq_spec, seq_spec]
        + [page_spec(g) for g in range(groups)] * 2
        + [pl.BlockSpec((rows, 1), lambda b, p, pt: (0, 0))],
        out_specs=pl.BlockSpec((None, rows, HEAD_DIM), lambda b, p, pt: (b, 0, 0)),
        scratch_shapes=[
            pltpu.VMEM((rows, D_MODEL), BF16),
            pltpu.VMEM((rows, D_MODEL), F32),
            pltpu.VMEM((rows, 1), F32),
        ],
    )
    return pl.pallas_call(
        functools.partial(_sbs_kernel, n_steps=n_steps, n_tok=n_tok),
        grid_spec=grid_spec,
        out_shape=jax.ShapeDtypeStruct((bsz, rows, HEAD_DIM), F32),
        compiler_params=_params("arbitrary", "arbitrary"),
    )(page_table, q, k_new, v_new, ck, ck, ck, ck, cv, cv, cv, cv, bias_col)


def _mix_kernel(u_ref, v_ref, g_ref, w_ref, bt_ref, o_ref, *vn_out):
    vn = _rms(v_ref[...], g_ref[...])
    if vn_out:
        vn_out[0][...] = vn
    vnb = vn.astype(BF16)
    r = lax.broadcasted_iota(jnp.int32, (CHUNK, CHUNK), 0)
    c = lax.broadcasted_iota(jnp.int32, (CHUNK, CHUNK), 1)
    lower = c <= r
    for g in range(CM_GROUPS):
        lo, hi = g * CM_GROUP_DIM, (g + 1) * CM_GROUP_DIM
        w = jnp.where(lower, w_ref[g], 0.0).astype(BF16)
        mixed = jnp.dot(w, vnb[:, lo:hi], preferred_element_type=F32) + bt_ref[:, g:g + 1]
        o_ref[:, lo:hi] = (u_ref[:, lo:hi] * mixed).astype(o_ref.dtype)


def _gmlp_mix(z, g_v, w_s, b_s, layer, want_v):
    m = z.shape[0]
    bt = jnp.swapaxes(b_s, 1, 2)
    row = pl.BlockSpec((CHUNK, D_MODEL), lambda c: (c, 0))
    out_specs = [row]
    out_shape = [jax.ShapeDtypeStruct((m, D_MODEL), BF16)]
    if want_v:
        out_specs.append(row)
        out_shape.append(jax.ShapeDtypeStruct((m, D_MODEL), F32))
    return pl.pallas_call(
        _mix_kernel,
        grid=(m // CHUNK,),
        in_specs=[
            row,
            pl.BlockSpec((CHUNK, D_MODEL), lambda c: (c, 1)),
            pl.BlockSpec((None, 1, D_MODEL), lambda c: (layer, 0, 0)),
            pl.BlockSpec((None, CM_GROUPS, CHUNK, CHUNK), lambda c: (layer, 0, 0, 0)),
            pl.BlockSpec((None, CHUNK, CM_GROUPS), lambda c: (layer, 0, 0)),
        ],
        out_specs=out_specs,
        out_shape=out_shape,
        compiler_params=_params("arbitrary"),
    )(z, z, g_v.reshape(g_v.shape[0], 1, D_MODEL), w_s, bt)


def kernel(x_prompt, x_sample, cache_k, cache_v, page_table, p_prompt, p_sample, g_mix_pre, g_mix_post, g_ffn_pre, g_ffn_post, w_qkv, w_o_sb, b_sb, w_uv, g_v, w_s, b_s, w_o_cm, w_up, w_down, w_pl_proj, g_pl, w_pl_gate):
    bp, sp, _ = x_prompt.shape
    bs, ss, _ = x_sample.shape
    streams = [
        dict(x=x_prompt.reshape(bp * sp, D_MODEL), p=p_prompt.reshape(DEPTH, bp * sp, PLE_DIM), prompt=True),
        dict(x=x_sample.reshape(bs * ss, D_MODEL), p=p_sample.reshape(DEPTH, bs * ss, PLE_DIM), prompt=False),
    ]
    for st in streams:
        (st["h"],) = _row_call(_norm_kernel, [st["x"]], [(g_mix_pre, 0)], [BF16])
        st["k"], st["v"] = [], []
    cm_rows = []

    for i in range(DEPTH):
        j = i // 2
        for st in streams:
            x, h = st["x"], st["h"]
            if i % 2 == 0:
                qkv32, qkv16 = _mm(h, w_qkv, j, "dual")
                st["k"].append(qkv32[:, D_MODEL:2 * D_MODEL])
                st["v"].append(qkv32[:, 2 * D_MODEL:])
                if st["prompt"]:
                    att = _sb_prompt(qkv16, b_sb[j], bp, sp)
                else:
                    pad = ((0, 0), (0, PAGE_SIZE - ss), (0, 0))
                    k_new = jnp.pad(qkv16[:, D_MODEL:2 * D_MODEL].reshape(bs, ss, D_MODEL), pad)
                    v_new = jnp.pad(qkv16[:, 2 * D_MODEL:].reshape(bs, ss, D_MODEL), pad)
                    q = qkv32[:, :D_MODEL].reshape(bs, ss, D_MODEL)
                    bias_col = jnp.tile(b_sb[j], ss).reshape(ss * N_HEADS, 1)
                    att = _sb_sample(q, k_new, v_new, cache_k, cache_v, page_table, bias_col, j)
                    att = att.reshape(bs * ss, D_MODEL).astype(BF16)
                mixed = _mm(att, w_o_sb, j, "f32")
            else:
                z = _mm(h, w_uv, j, "gelu")
                if st["prompt"]:
                    (gated,) = _gmlp_mix(z, g_v, w_s, b_s, j, False)
                else:
                    zp = jnp.pad(z.reshape(bs, ss, 2 * D_MODEL), ((0, 0), (0, CHUNK - ss), (0, 0)))
                    gated, vn = _gmlp_mix(zp.reshape(bs * CHUNK, 2 * D_MODEL), g_v, w_s, b_s, j, True)
                    gated = gated.reshape(bs, CHUNK, D_MODEL)[:, :ss].reshape(bs * ss, D_MODEL)
                    cm_rows.append(vn.reshape(bs, CHUNK, D_MODEL)[:, :ss])
                mixed = _mm(gated, w_o_cm, j, "f32")
            x1, h_ffn = _row_call(_resnorm_kernel, [x, mixed], [(g_mix_post, i), (g_ffn_pre, i)], [F32, BF16])
            act = _mm(h_ffn, w_up, i, "relu2", out_dtype=BF16)
            f = _mm(act, w_down, i, "f32")
            x2, h_pl = _row_call(_resnorm_kernel, [x1, f], [(g_ffn_post, i), (g_pl, i)], [F32, BF16])
            gated_ple = _mm(h_pl, w_pl_gate, i, "ple", p=st["p"][i], wp_stack=w_pl_proj)
            if i + 1 < DEPTH:
                st["x"], st["h"] = _row_call(_addnorm_kernel, [x2, gated_ple], [(g_mix_pre, i + 1)], [F32, BF16])
            else:
                (st["x"],) = _row_call(_add_kernel, [x2, gated_ple], [], [F32])

    sp_st, ss_st = streams

    def heads(rows, b, s):
        return jnp.stack([r.reshape(b, s, N_HEADS, HEAD_DIM) for r in rows], axis=2)

    return (
        sp_st["x"].reshape(bp, sp, D_MODEL),
        ss_st["x"].reshape(bs, ss, D_MODEL),
        heads(sp_st["k"], bp, sp),
        heads(sp_st["v"], bp, sp),
        heads(ss_st["k"], bs, ss),
        heads(ss_st["v"], bs, ss),
        jnp.stack(cm_rows, axis=2),
    )
```

```python
import functools

import jax
import jax.numpy as jnp
from jax import lax
from jax.experimental import pallas as pl
from jax.experimental.pallas import tpu as pltpu

D_MODEL = 4096
DEPTH = 4
N_SB = 2
N_HEADS = 32
HEAD_DIM = D_MODEL // N_HEADS
PAGE_SIZE = 128
CHUNK = 128
CM_GROUPS = 16
CM_GROUP_DIM = D_MODEL // CM_GROUPS
PLE_DIM = 256
EPS = 1e-6

V7X_VMEM_LIMIT_BYTES = 56 * 1024 * 1024
HEADS_PER_TILE = 8

MM_TM, MM_TN, MM_TK = 2048, 1024, 1024

F32 = jnp.float32
BF16 = jnp.bfloat16


def _params(*sem):
    return pltpu.CompilerParams(dimension_semantics=sem, vmem_limit_bytes=V7X_VMEM_LIMIT_BYTES)


def _rms(x, g):
    ms = jnp.mean(x * x, axis=-1, keepdims=True)
    return x * lax.rsqrt(ms + EPS) * g


def _norm_kernel(x_ref, g_ref, h_ref):
    h_ref[...] = _rms(x_ref[...], g_ref[...]).astype(BF16)


def _resnorm_kernel(x_ref, m_ref, gp_ref, gn_ref, x1_ref, h_ref):
    x1 = x_ref[...] + _rms(m_ref[...], gp_ref[...])
    x1_ref[...] = x1
    h_ref[...] = _rms(x1, gn_ref[...]).astype(BF16)


def _addnorm_kernel(x_ref, a_ref, gn_ref, x1_ref, h_ref):
    x1 = x_ref[...] + a_ref[...]
    x1_ref[...] = x1
    h_ref[...] = _rms(x1, gn_ref[...]).astype(BF16)


def _add_kernel(x_ref, a_ref, x1_ref):
    x1_ref[...] = x_ref[...] + a_ref[...]


def _row_call(body, arrays, gains, out_dtypes):
    m = arrays[0].shape[0]
    tr = min(m, 256)
    row = pl.BlockSpec((tr, D_MODEL), lambda r: (r, 0))
    in_specs = [row] * len(arrays)
    args = list(arrays)
    for table, layer in gains:
        in_specs.append(pl.BlockSpec((None, 1, D_MODEL), lambda r, layer=layer: (layer, 0, 0)))
        args.append(table.reshape(table.shape[0], 1, D_MODEL))
    return pl.pallas_call(
        body,
        name=body.__name__.strip("_"),
        grid=(m // tr,),
        in_specs=in_specs,
        out_specs=[row] * len(out_dtypes),
        out_shape=[jax.ShapeDtypeStruct((m, D_MODEL), dt) for dt in out_dtypes],
        compiler_params=_params("arbitrary"),
    )(*args)


def _gelu_tanh(x):
    c = 0.7978845608028654
    return 0.5 * x * (1.0 + jnp.tanh(c * (x + 0.044715 * (x * x * x))))


_ACC_IN_OUTPUT = ("f32", "kv", "ple")


def _mm_kernel(*refs, epi, nk, n_alias):
    it = iter(refs)
    a_refs = (next(it), next(it))
    w_ref = next(it)
    p_refs = (None, None)
    wp_ref = None
    if epi == "ple":
        p_refs = (next(it), next(it))
        wp_ref = next(it)
    for _ in range(n_alias):
        next(it)
    outs = []
    for _ in range(2):
        o = next(it)
        ob = next(it) if epi == "kv" else None
        outs.append((o, ob))
    accs = [o for o, _ in outs] if epi in _ACC_IN_OUTPUT else [next(it), next(it)]

    i = pl.program_id(1)
    k = pl.program_id(2)
    wb = w_ref[...].astype(BF16)

    def finish(v, o_ref, ob_ref, p_ref):
        if epi == "kv":
            o_ref[...] = v
            ob_ref[...] = v.astype(BF16)
        elif epi == "relu2":
            r = jnp.maximum(v, 0.0)
            o_ref[...] = (r * r).astype(o_ref.dtype)
        elif epi == "gelu":
            o_ref[...] = _gelu_tanh(v).astype(o_ref.dtype)
        elif epi == "ple":
            pp = jnp.dot(p_ref[...].astype(BF16), wp_ref[...].astype(BF16), preferred_element_type=F32)
            o_ref[...] = jax.nn.sigmoid(v) * pp
        else:
            o_ref[...] = v.astype(o_ref.dtype)

    def stream(s, active):
        a_ref, acc_ref = a_refs[s], accs[s]
        o_ref, ob_ref = outs[s]

        def prod():
            return jnp.dot(a_ref[...], wb, preferred_element_type=F32)

        @pl.when(active & (k == 0))
        def _():
            acc_ref[...] = prod()

        @pl.when(active & (k > 0) & (k < nk - 1))
        def _():
            acc_ref[...] += prod()

        @pl.when(active & (k == nk - 1))
        def _():
            finish(acc_ref[...] + prod(), o_ref, ob_ref, p_refs[s])

    stream(0, True)
    stream(1, i == 0)


def _mm(a, a_s, w_stack, layer, epi, *, w_col0=0, n=None, out_dtype=F32,
        p=None, p_s=None, wp_stack=None, kv_bufs=None, kv_col0=0, kv_cols=None):
    m, kdim = a.shape
    ms = a_s.shape[0]
    n = w_stack.shape[2] if n is None else n
    tm, tn, tk = MM_TM, MM_TN, MM_TK
    nk = kdim // tk
    assert m % tm == 0 and n % tn == 0 and kdim % tk == 0 and nk >= 2
    assert w_col0 % tn == 0 and kv_col0 % tn == 0
    wc, kc = w_col0 // tn, kv_col0 // tn
    grid = (n // tn, m // tm, nk)
    in_specs = [
        pl.BlockSpec((tm, tk), lambda j, i, k: (i, k)),
        pl.BlockSpec((ms, tk), lambda j, i, k: (0, k)),
        pl.BlockSpec((None, tk, tn), lambda j, i, k: (layer, k, wc + j)),
    ]
    args = [a, a_s, w_stack]
    if epi == "ple":
        in_specs += [
            pl.BlockSpec((tm, PLE_DIM), lambda j, i, k: (i, 0)),
            pl.BlockSpec((ms, PLE_DIM), lambda j, i, k: (0, 0)),
            pl.BlockSpec((None, PLE_DIM, tn), lambda j, i, k: (layer, 0, j)),
        ]
        args += [p, p_s, wp_stack]
    aliases = {}
    n_alias = 0
    if kv_bufs is not None:
        n_alias = 2
        for t, buf in enumerate(kv_bufs):
            aliases[len(args)] = 2 * t
            in_specs.append(pl.BlockSpec(memory_space=pl.ANY))
            args.append(buf)
    out_specs, out_shape = [], []
    for rows, tr, row_index in ((m, tm, lambda i: i), (ms, ms, lambda i: 0)):
        if epi == "kv":
            out_specs.append(pl.BlockSpec((tr, tn), lambda j, i, k, f=row_index: (f(i), kc + j)))
            out_shape.append(jax.ShapeDtypeStruct((rows, kv_cols), F32))
            out_specs.append(pl.BlockSpec((tr, tn), lambda j, i, k, f=row_index: (f(i), j)))
            out_shape.append(jax.ShapeDtypeStruct((rows, n), BF16))
        else:
            out_specs.append(pl.BlockSpec((tr, tn), lambda j, i, k, f=row_index: (f(i), j)))
            out_shape.append(jax.ShapeDtypeStruct((rows, n), out_dtype))
    scratch = []
    if epi not in _ACC_IN_OUTPUT:
        scratch = [pltpu.VMEM((tm, tn), F32), pltpu.VMEM((ms, tn), F32)]
    outs = pl.pallas_call(
        functools.partial(_mm_kernel, epi=epi, nk=nk, n_alias=n_alias),
        name=f"mm_{epi}_k{kdim}_n{n}",
        grid=grid,
        in_specs=in_specs,
        out_specs=out_specs,
        out_shape=out_shape,
        scratch_shapes=scratch,
        input_output_aliases=aliases,
        compiler_params=_params("arbitrary", "arbitrary", "arbitrary"),
    )(*args)
    half = len(outs) // 2
    return tuple(outs[:half]), tuple(outs[half:])


def _softplus(z):
    return jnp.maximum(z, 0.0) + jnp.log(1.0 + jnp.exp(-jnp.abs(z)))


def _suffix_sum_lanes(x, tri2):
    hi = x.astype(BF16)
    lo = (x - hi.astype(F32)).astype(BF16)
    return jnp.dot(jnp.concatenate([hi, lo], axis=1), tri2, preferred_element_type=F32)


def _sb_scores(q, kt):
    return lax.dot_general(q, kt, (((1,), (1,)), ((), ())), preferred_element_type=F32)


def _sb_keep(s, bias, tri2, carry, valid):
    z = s * (HEAD_DIM ** -0.5) + bias
    sp = _softplus(z)
    if valid is not None:
        sp = jnp.where(valid, sp, 0.0)
    later = _suffix_sum_lanes(sp, tri2) + carry
    return z - sp, later, later[:, 0:1] + sp[:, 0:1]


def _sb_weights(log_beta, later, valid):
    w = jnp.exp(log_beta - later)
    if valid is not None:
        w = jnp.where(valid, w, 0.0)
    return w.astype(BF16)


def _sb_tile(q, kt, vt, bias, tri2, carry, acc, valid):
    log_beta, later, carry = _sb_keep(_sb_scores(q, kt), bias, tri2, carry, valid)
    acc = acc + jnp.dot(_sb_weights(log_beta, later, valid), vt, preferred_element_type=F32)
    return carry, acc


def _strict_upper2(n):
    r = lax.broadcasted_iota(jnp.int32, (2 * n, n), 0) % n
    c = lax.broadcasted_iota(jnp.int32, (2 * n, n), 1)
    return (r > c).astype(BF16)


SB_TILE = 256
SB_HEADS = 4


def _sbp_kernel(bias_ref, q_ref, k_ref, v_ref, o_ref, acc_ref, car_ref):
    hg = pl.program_id(1)
    qi = pl.program_id(2)
    t = SB_TILE
    tri = _strict_upper2(t)
    r = lax.broadcasted_iota(jnp.int32, (t, t), 0)
    c = lax.broadcasted_iota(jnp.int32, (t, t), 1)
    causal = c < r

    def sweep(kj, valid, first):
        start = pl.multiple_of(kj * t, t)
        heads = range(SB_HEADS)
        lanes = [slice(hh * HEAD_DIM, (hh + 1) * HEAD_DIM) for hh in heads]
        scores = [_sb_scores(q_ref[:, lanes[hh]], k_ref[pl.ds(start, t), lanes[hh]]) for hh in heads]
        keeps = []
        for hh in heads:
            carry = jnp.zeros((t, 1), F32) if first else car_ref[hh]
            log_beta, later, carry = _sb_keep(scores[hh], bias_ref[hg * SB_HEADS + hh], tri, carry, valid)
            car_ref[hh] = carry
            keeps.append((log_beta, later))
        for hh in heads:
            w = _sb_weights(*keeps[hh], valid)
            pv = jnp.dot(w, v_ref[pl.ds(start, t), lanes[hh]], preferred_element_type=F32)
            acc_ref[hh] = pv if first else acc_ref[hh] + pv

    sweep(qi, causal, True)

    def body(i, _):
        sweep(qi - 1 - i, None, False)
        return 0

    lax.fori_loop(0, qi, body, 0)
    for hh in range(SB_HEADS):
        o_ref[:, hh * HEAD_DIM:(hh + 1) * HEAD_DIM] = acc_ref[hh].astype(o_ref.dtype)


def _sb_prompt(q, k, v, bias, batch, seq):
    t = SB_TILE
    nq = seq // t
    width = SB_HEADS * HEAD_DIM
    q_spec = pl.BlockSpec((t, width), lambda b, hg, qi: (b * nq + qi, hg))
    kv_spec = pl.BlockSpec((seq, width), lambda b, hg, qi: (b, hg))
    return pl.pallas_call(
        _sbp_kernel,
        name="sb_prompt",
        grid=(batch, N_HEADS // SB_HEADS, nq),
        in_specs=[pl.BlockSpec(memory_space=pltpu.SMEM), q_spec, kv_spec, kv_spec],
        out_specs=q_spec,
        out_shape=jax.ShapeDtypeStruct((batch * seq, D_MODEL), BF16),
        scratch_shapes=[
            pltpu.VMEM((SB_HEADS, t, HEAD_DIM), F32),
            pltpu.VMEM((SB_HEADS, t, 1), F32),
        ],
        compiler_params=_params("arbitrary", "arbitrary", "arbitrary"),
    )(bias, q, k, v)


def _sbs_kernel(pt_ref, q_ref, kn_ref, vn_ref, k0, k1, k2, k3, v0, v1, v2, v3, bias_ref,
                o_ref, qbd_ref, acc_ref, car_ref, *, n_steps, n_tok):
    del pt_ref
    p = pl.program_id(1)
    rows = n_tok * N_HEADS
    tri = _strict_upper2(PAGE_SIZE)
    bias = bias_ref[...]

    def process(kcat, vcat, valid):
        carry, acc = _sb_tile(qbd_ref[...], kcat, vcat, bias, tri, car_ref[...], acc_ref[...], valid)
        car_ref[...] = carry
        acc_ref[...] = acc

    @pl.when(p == 0)
    def _():
        lane_head = lax.broadcasted_iota(jnp.int32, (N_HEADS, D_MODEL), 1) // HEAD_DIM
        row_head = lax.broadcasted_iota(jnp.int32, (N_HEADS, D_MODEL), 0)
        own = lane_head == row_head
        for t in range(n_tok):
            qrow = q_ref[t:t + 1, :]
            qbd_ref[t * N_HEADS:(t + 1) * N_HEADS, :] = jnp.where(own, qrow, 0.0).astype(BF16)
        acc_ref[...] = jnp.zeros_like(acc_ref)
        car_ref[...] = jnp.zeros_like(car_ref)
        tok = lax.broadcasted_iota(jnp.int32, (rows, PAGE_SIZE), 0) // N_HEADS
        key = lax.broadcasted_iota(jnp.int32, (rows, PAGE_SIZE), 1)
        process(kn_ref[...], vn_ref[...], key < tok)

    @pl.when(p > 0)
    def _():
        def cat(parts):
            tiles = []
            for ref in parts:
                flat = ref.reshape(PAGE_SIZE * HEADS_PER_TILE, HEAD_DIM)
                for hh in range(HEADS_PER_TILE):
                    tiles.append(flat[pl.ds(hh, PAGE_SIZE, stride=HEADS_PER_TILE), :])
            return jnp.concatenate(tiles, axis=1).astype(BF16)

        process(cat((k0, k1, k2, k3)), cat((v0, v1, v2, v3)), None)

    @pl.when(p == n_steps - 1)
    def _():
        row_head = lax.broadcasted_iota(jnp.int32, (rows, HEAD_DIM), 0) % N_HEADS
        out = jnp.zeros((rows, HEAD_DIM), F32)
        for h in range(N_HEADS):
            out = out + jnp.where(row_head == h, acc_ref[:, h * HEAD_DIM:(h + 1) * HEAD_DIM], 0.0)
        o_ref[...] = out


def _sb_sample(q, k_new, v_new, cache_k, cache_v, page_table, bias_col, layer):
    bsz, n_tok, _ = q.shape
    n_pages = page_table.shape[1]
    n_pool = cache_k.shape[0]
    n_layers = cache_k.shape[2]
    groups = N_HEADS // HEADS_PER_TILE
    rows = n_tok * N_HEADS
    n_steps = n_pages + 1
    ck = cache_k.reshape(n_pool, PAGE_SIZE, n_layers, groups, HEADS_PER_TILE, HEAD_DIM)
    cv = cache_v.reshape(n_pool, PAGE_SIZE, n_layers, groups, HEADS_PER_TILE, HEAD_DIM)

    def page_spec(g):
        def index(b, p, pt):
            page = pt[b, n_pages - 1 - jnp.maximum(p - 1, 0)]
            return (page, 0, layer, g, 0, 0)
        return pl.BlockSpec((None, PAGE_SIZE, None, None, HEADS_PER_TILE, HEAD_DIM), index)

    seq_spec = pl.BlockSpec((None, PAGE_SIZE, D_MODEL), lambda b, p, pt: (b, 0, 0))
    grid_spec = pltpu.PrefetchScalarGridSpec(
        num_scalar_prefetch=1,
        grid=(bsz, n_steps),
        in_specs=[pl.BlockSpec((None, n_tok, D_MODEL), lambda b, p, pt: (b, 0, 0)), seq_spec, seq_spec]
        + [page_spec(g) for g in range(groups)] * 2
        + [pl.BlockSpec((rows, 1), lambda b, p, pt: (0, 0))],
        out_specs=pl.BlockSpec((None, rows, HEAD_DIM), lambda b, p, pt: (b, 0, 0)),
        scratch_shapes=[
            pltpu.VMEM((rows, D_MODEL), BF16),
            pltpu.VMEM((rows, D_MODEL), F32),
            pltpu.VMEM((rows, 1), F32),
        ],
    )
    return pl.pallas_call(
        functools.partial(_sbs_kernel, n_steps=n_steps, n_tok=n_tok),
        name="sb_sample",
        grid_spec=grid_spec,
        out_shape=jax.ShapeDtypeStruct((bsz, rows, HEAD_DIM), F32),
        compiler_params=_params("arbitrary", "arbitrary"),
    )(page_table, q, k_new, v_new, ck, ck, ck, ck, cv, cv, cv, cv, bias_col)


def _mix_kernel(u_ref, v_ref, g_ref, w_ref, bt_ref, o_ref, *vn_out):
    vn = _rms(v_ref[...], g_ref[...])
    if vn_out:
        vn_out[0][...] = vn
    vnb = vn.astype(BF16)
    r = lax.broadcasted_iota(jnp.int32, (CHUNK, CHUNK), 0)
    c = lax.broadcasted_iota(jnp.int32, (CHUNK, CHUNK), 1)
    lower = c <= r
    for g in range(CM_GROUPS):
        lo, hi = g * CM_GROUP_DIM, (g + 1) * CM_GROUP_DIM
        w = jnp.where(lower, w_ref[g], 0.0).astype(BF16)
        mixed = jnp.dot(w, vnb[:, lo:hi], preferred_element_type=F32) + bt_ref[:, g:g + 1]
        o_ref[:, lo:hi] = (u_ref[:, lo:hi] * mixed).astype(o_ref.dtype)


def _gmlp_mix(z, g_v, w_s, b_s, layer, want_v):
    m = z.shape[0]
    bt = jnp.swapaxes(b_s, 1, 2)
    row = pl.BlockSpec((CHUNK, D_MODEL), lambda c: (c, 0))
    out_specs = [row]
    out_shape = [jax.ShapeDtypeStruct((m, D_MODEL), BF16)]
    if want_v:
        out_specs.append(row)
        out_shape.append(jax.ShapeDtypeStruct((m, D_MODEL), F32))
    return pl.pallas_call(
        _mix_kernel,
        name="gmlp_mix",
        grid=(m // CHUNK,),
        in_specs=[
            row,
            pl.BlockSpec((CHUNK, D_MODEL), lambda c: (c, 1)),
            pl.BlockSpec((None, 1, D_MODEL), lambda c: (layer, 0, 0)),
            pl.BlockSpec((None, CM_GROUPS, CHUNK, CHUNK), lambda c: (layer, 0, 0, 0)),
            pl.BlockSpec((None, CHUNK, CM_GROUPS), lambda c: (layer, 0, 0)),
        ],
        out_specs=out_specs,
        out_shape=out_shape,
        compiler_params=_params("arbitrary"),
    )(z, z, g_v.reshape(g_v.shape[0], 1, D_MODEL), w_s, bt)


def kernel(x_prompt, x_sample, cache_k, cache_v, page_table, p_prompt, p_sample, g_mix_pre, g_mix_post, g_ffn_pre, g_ffn_post, w_qkv, w_o_sb, b_sb, w_uv, g_v, w_s, b_s, w_o_cm, w_up, w_down, w_pl_proj, g_pl, w_pl_gate):
    bp, sp, _ = x_prompt.shape
    bs, ss, _ = x_sample.shape
    mp, msm = bp * sp, bs * ss
    xs = [x_prompt.reshape(mp, D_MODEL), x_sample.reshape(msm, D_MODEL)]
    ps = [p_prompt.reshape(DEPTH, mp, PLE_DIM), p_sample.reshape(DEPTH, msm, PLE_DIM)]
    hs = [_row_call(_norm_kernel, [x], [(g_mix_pre, 0)], [BF16])[0] for x in xs]
    kv_cols = N_SB * D_MODEL
    k_bufs = v_bufs = None
    cm_rows = []

    for i in range(DEPTH):
        j = i // 2
        if i % 2 == 0:
            (q_p,), (q_s,) = _mm(hs[0], hs[1], w_qkv, j, "bf16", n=D_MODEL, out_dtype=BF16)
            (kb_p, k_p), (kb_s, k_s) = _mm(hs[0], hs[1], w_qkv, j, "kv", w_col0=D_MODEL, n=D_MODEL,
                                           kv_bufs=k_bufs, kv_col0=j * D_MODEL, kv_cols=kv_cols)
            (vb_p, v_p), (vb_s, v_s) = _mm(hs[0], hs[1], w_qkv, j, "kv", w_col0=2 * D_MODEL, n=D_MODEL,
                                           kv_bufs=v_bufs, kv_col0=j * D_MODEL, kv_cols=kv_cols)
            k_bufs, v_bufs = (kb_p, kb_s), (vb_p, vb_s)
            att_p = _sb_prompt(q_p, k_p, v_p, b_sb[j], bp, sp)
            pad = ((0, 0), (0, PAGE_SIZE - ss), (0, 0))
            k_new = jnp.pad(k_s.reshape(bs, ss, D_MODEL), pad)
            v_new = jnp.pad(v_s.reshape(bs, ss, D_MODEL), pad)
            bias_col = jnp.tile(b_sb[j], ss).reshape(ss * N_HEADS, 1)
            att_s = _sb_sample(q_s.astype(F32).reshape(bs, ss, D_MODEL), k_new, v_new,
                               cache_k, cache_v, page_table, bias_col, j)
            att_s = att_s.reshape(msm, D_MODEL).astype(BF16)
            (mix_p,), (mix_s,) = _mm(att_p, att_s, w_o_sb, j, "f32")
        else:
            (z_p,), (z_s,) = _mm(hs[0], hs[1], w_uv, j, "gelu")
            (gated_p,) = _gmlp_mix(z_p, g_v, w_s, b_s, j, False)
            zp = jnp.pad(z_s.reshape(bs, ss, 2 * D_MODEL), ((0, 0), (0, CHUNK - ss), (0, 0)))
            gated_s, vn = _gmlp_mix(zp.reshape(bs * CHUNK, 2 * D_MODEL), g_v, w_s, b_s, j, True)
            gated_s = gated_s.reshape(bs, CHUNK, D_MODEL)[:, :ss].reshape(msm, D_MODEL)
            cm_rows.append(vn.reshape(bs, CHUNK, D_MODEL)[:, :ss])
            (mix_p,), (mix_s,) = _mm(gated_p, gated_s, w_o_cm, j, "f32")
        x1s, hfs = zip(*[_row_call(_resnorm_kernel, [x, m], [(g_mix_post, i), (g_ffn_pre, i)], [F32, BF16])
                         for x, m in zip(xs, (mix_p, mix_s))])
        (act_p,), (act_s,) = _mm(hfs[0], hfs[1], w_up, i, "relu2", out_dtype=BF16)
        (f_p,), (f_s,) = _mm(act_p, act_s, w_down, i, "f32")
        x2s, hps = zip(*[_row_call(_resnorm_kernel, [x, f], [(g_ffn_post, i), (g_pl, i)], [F32, BF16])
                         for x, f in zip(x1s, (f_p, f_s))])
        (gp_p,), (gp_s,) = _mm(hps[0], hps[1], w_pl_gate, i, "ple", p=ps[0][i], p_s=ps[1][i],
                                wp_stack=w_pl_proj)
        if i + 1 < DEPTH:
            xs, hs = zip(*[_row_call(_addnorm_kernel, [x, g], [(g_mix_pre, i + 1)], [F32, BF16])
                           for x, g in zip(x2s, (gp_p, gp_s))])
        else:
            xs = [_row_call(_add_kernel, [x, g], [], [F32])[0] for x, g in zip(x2s, (gp_p, gp_s))]

    def heads(buf, b, s):
        return buf.reshape(b, s, N_SB, N_HEADS, HEAD_DIM)

    return (
        xs[0].reshape(bp, sp, D_MODEL),
        xs[1].reshape(bs, ss, D_MODEL),
        heads(k_bufs[0], bp, sp),
        heads(v_bufs[0], bp, sp),
        heads(k_bufs[1], bs, ss),
        heads(v_bufs[1], bs, ss),
        jnp.stack(cm_rows, axis=2),
    )
```

```python
import functools

import jax
import jax.numpy as jnp
from jax import lax
from jax.experimental import pallas as pl
from jax.experimental.pallas import tpu as pltpu

D_MODEL = 4096
DEPTH = 4
N_SB = 2
N_HEADS = 32
HEAD_DIM = D_MODEL // N_HEADS
PAGE_SIZE = 128
CHUNK = 128
CM_GROUPS = 16
CM_GROUP_DIM = D_MODEL // CM_GROUPS
PLE_DIM = 256
EPS = 1e-6

V7X_VMEM_LIMIT_BYTES = 56 * 1024 * 1024
HEADS_PER_TILE = 8

MM_TM, MM_TN, MM_TK = 2048, 1024, 1024
MM_CHUNK = 256

F32 = jnp.float32
BF16 = jnp.bfloat16


def _params(*sem):
    return pltpu.CompilerParams(dimension_semantics=sem, vmem_limit_bytes=V7X_VMEM_LIMIT_BYTES)


def _rms(x, g):
    ms = jnp.mean(x * x, axis=-1, keepdims=True)
    return x * lax.rsqrt(ms + EPS) * g


def _norm_kernel(x_ref, g_ref, h_ref):
    h_ref[...] = _rms(x_ref[...], g_ref[...]).astype(BF16)


def _resnorm_kernel(x_ref, m_ref, gp_ref, gn_ref, x1_ref, h_ref):
    x1 = x_ref[...] + _rms(m_ref[...], gp_ref[...])
    x1_ref[...] = x1
    h_ref[...] = _rms(x1, gn_ref[...]).astype(BF16)


def _addnorm_kernel(x_ref, a_ref, gn_ref, x1_ref, h_ref):
    x1 = x_ref[...] + a_ref[...]
    x1_ref[...] = x1
    h_ref[...] = _rms(x1, gn_ref[...]).astype(BF16)


def _add_kernel(x_ref, a_ref, x1_ref):
    x1_ref[...] = x_ref[...] + a_ref[...]


def _row_call(body, arrays, gains, out_dtypes):
    m = arrays[0].shape[0]
    tr = min(m, 256)
    row = pl.BlockSpec((tr, D_MODEL), lambda r: (r, 0))
    in_specs = [row] * len(arrays)
    args = list(arrays)
    for table, layer in gains:
        in_specs.append(pl.BlockSpec((None, 1, D_MODEL), lambda r, layer=layer: (layer, 0, 0)))
        args.append(table.reshape(table.shape[0], 1, D_MODEL))
    return pl.pallas_call(
        body,
        name=body.__name__.strip("_"),
        grid=(m // tr,),
        in_specs=in_specs,
        out_specs=[row] * len(out_dtypes),
        out_shape=[jax.ShapeDtypeStruct((m, D_MODEL), dt) for dt in out_dtypes],
        compiler_params=_params("arbitrary"),
    )(*args)


def _gelu_tanh(x):
    c = 0.7978845608028654
    return 0.5 * x * (1.0 + jnp.tanh(c * (x + 0.044715 * (x * x * x))))


_ACC_IN_OUTPUT = ("f32", "kv", "ple")


def _mm_kernel(*refs, epi, nk, n_alias):
    it = iter(refs)
    a_refs = (next(it), next(it))
    w_ref = next(it)
    p_refs = (None, None)
    wp_ref = None
    if epi == "ple":
        p_refs = (next(it), next(it))
        wp_ref = next(it)
    for _ in range(n_alias):
        next(it)
    outs = []
    for _ in range(2):
        o = next(it)
        ob = next(it) if epi == "kv" else None
        outs.append((o, ob))
    accs = [o for o, _ in outs] if epi in _ACC_IN_OUTPUT else [next(it), next(it)]

    i = pl.program_id(1)
    k = pl.program_id(2)

    def finish(v, o_ref, ob_ref, p_ref):
        if epi == "kv":
            o_ref[...] = v
            ob_ref[...] = v.astype(BF16)
        elif epi == "relu2":
            r = jnp.maximum(v, 0.0)
            o_ref[...] = (r * r).astype(o_ref.dtype)
        elif epi == "gelu":
            o_ref[...] = _gelu_tanh(v).astype(o_ref.dtype)
        elif epi == "ple":
            pp = jnp.dot(p_ref[...].astype(BF16), wp_ref[...].astype(BF16), preferred_element_type=F32)
            o_ref[...] = jax.nn.sigmoid(v) * pp
        else:
            o_ref[...] = v.astype(o_ref.dtype)

    def stream(s, active):
        a_ref, acc_ref = a_refs[s], accs[s]
        o_ref, ob_ref = outs[s]

        def prod():
            a = a_ref[...]
            tn = w_ref.shape[1]
            parts = [jnp.dot(a, w_ref[:, c:c + MM_CHUNK].astype(BF16), preferred_element_type=F32)
                     for c in range(0, tn, MM_CHUNK)]
            return jnp.concatenate(parts, axis=1)

        @pl.when(active & (k == 0))
        def _():
            acc_ref[...] = prod()

        @pl.when(active & (k > 0) & (k < nk - 1))
        def _():
            acc_ref[...] += prod()

        @pl.when(active & (k == nk - 1))
        def _():
            finish(acc_ref[...] + prod(), o_ref, ob_ref, p_refs[s])

    stream(0, True)
    stream(1, i == 0)


def _mm(a, a_s, w_stack, layer, epi, *, w_col0=0, n=None, out_dtype=F32,
        p=None, p_s=None, wp_stack=None, kv_bufs=None, kv_col0=0, kv_cols=None):
    m, kdim = a.shape
    ms = a_s.shape[0]
    n = w_stack.shape[2] if n is None else n
    tm, tn, tk = MM_TM, MM_TN, MM_TK
    nk = kdim // tk
    assert m % tm == 0 and n % tn == 0 and kdim % tk == 0 and nk >= 2
    assert w_col0 % tn == 0 and kv_col0 % tn == 0
    wc, kc = w_col0 // tn, kv_col0 // tn
    grid = (n // tn, m // tm, nk)
    in_specs = [
        pl.BlockSpec((tm, tk), lambda j, i, k: (i, k)),
        pl.BlockSpec((ms, tk), lambda j, i, k: (0, k)),
        pl.BlockSpec((None, tk, tn), lambda j, i, k: (layer, k, wc + j)),
    ]
    args = [a, a_s, w_stack]
    if epi == "ple":
        in_specs += [
            pl.BlockSpec((tm, PLE_DIM), lambda j, i, k: (i, 0)),
            pl.BlockSpec((ms, PLE_DIM), lambda j, i, k: (0, 0)),
            pl.BlockSpec((None, PLE_DIM, tn), lambda j, i, k: (layer, 0, j)),
        ]
        args += [p, p_s, wp_stack]
    aliases = {}
    n_alias = 0
    if kv_bufs is not None:
        n_alias = 2
        for t, buf in enumerate(kv_bufs):
            aliases[len(args)] = 2 * t
            in_specs.append(pl.BlockSpec(memory_space=pl.ANY))
            args.append(buf)
    out_specs, out_shape = [], []
    for rows, tr, row_index in ((m, tm, lambda i: i), (ms, ms, lambda i: 0)):
        if epi == "kv":
            out_specs.append(pl.BlockSpec((tr, tn), lambda j, i, k, f=row_index: (f(i), kc + j)))
            out_shape.append(jax.ShapeDtypeStruct((rows, kv_cols), F32))
            out_specs.append(pl.BlockSpec((tr, tn), lambda j, i, k, f=row_index: (f(i), j)))
            out_shape.append(jax.ShapeDtypeStruct((rows, n), BF16))
        else:
            out_specs.append(pl.BlockSpec((tr, tn), lambda j, i, k, f=row_index: (f(i), j)))
            out_shape.append(jax.ShapeDtypeStruct((rows, n), out_dtype))
    scratch = []
    if epi not in _ACC_IN_OUTPUT:
        scratch = [pltpu.VMEM((tm, tn), F32), pltpu.VMEM((ms, tn), F32)]
    outs = pl.pallas_call(
        functools.partial(_mm_kernel, epi=epi, nk=nk, n_alias=n_alias),
        name=f"mm_{epi}_k{kdim}_n{n}",
        grid=grid,
        in_specs=in_specs,
        out_specs=out_specs,
        out_shape=out_shape,
        scratch_shapes=scratch,
        input_output_aliases=aliases,
        compiler_params=_params("arbitrary", "arbitrary", "arbitrary"),
    )(*args)
    half = len(outs) // 2
    return tuple(outs[:half]), tuple(outs[half:])


def _softplus(z):
    return jnp.maximum(z, 0.0) + jnp.log(1.0 + jnp.exp(-jnp.abs(z)))


def _suffix_sum_lanes(x, tri2):
    hi = x.astype(BF16)
    lo = (x - hi.astype(F32)).astype(BF16)
    return jnp.dot(jnp.concatenate([hi, lo], axis=1), tri2, preferred_element_type=F32)


def _sb_scores(q, kt):
    return lax.dot_general(q, kt, (((1,), (1,)), ((), ())), preferred_element_type=F32)


def _sb_keep(s, bias, tri2, carry, valid):
    z = s * (HEAD_DIM ** -0.5) + bias
    sp = _softplus(z)
    if valid is not None:
        sp = jnp.where(valid, sp, 0.0)
    later = _suffix_sum_lanes(sp, tri2) + carry
    return z - sp, later, later[:, 0:1] + sp[:, 0:1]


def _sb_weights(log_beta, later, valid):
    w = jnp.exp(log_beta - later)
    if valid is not None:
        w = jnp.where(valid, w, 0.0)
    return w.astype(BF16)


def _sb_tile(q, kt, vt, bias, tri2, carry, acc, valid):
    log_beta, later, carry = _sb_keep(_sb_scores(q, kt), bias, tri2, carry, valid)
    acc = acc + jnp.dot(_sb_weights(log_beta, later, valid), vt, preferred_element_type=F32)
    return carry, acc


def _strict_upper2(n):
    r = lax.broadcasted_iota(jnp.int32, (2 * n, n), 0) % n
    c = lax.broadcasted_iota(jnp.int32, (2 * n, n), 1)
    return (r > c).astype(BF16)


SB_TILE = 256
SB_HEADS = 4


def _sbp_kernel(bias_ref, q_ref, k_ref, v_ref, o_ref, acc_ref, car_ref):
    hg = pl.program_id(1)
    qi = pl.program_id(2)
    t = SB_TILE
    tri = _strict_upper2(t)
    r = lax.broadcasted_iota(jnp.int32, (t, t), 0)
    c = lax.broadcasted_iota(jnp.int32, (t, t), 1)
    causal = c < r

    def sweep(kj, valid, first):
        start = pl.multiple_of(kj * t, t)
        heads = range(SB_HEADS)
        lanes = [slice(hh * HEAD_DIM, (hh + 1) * HEAD_DIM) for hh in heads]
        scores = [_sb_scores(q_ref[:, lanes[hh]], k_ref[pl.ds(start, t), lanes[hh]]) for hh in heads]
        keeps = []
        for hh in heads:
            carry = jnp.zeros((t, 1), F32) if first else car_ref[hh]
            log_beta, later, carry = _sb_keep(scores[hh], bias_ref[hg * SB_HEADS + hh], tri, carry, valid)
            car_ref[hh] = carry
            keeps.append((log_beta, later))
        for hh in heads:
            w = _sb_weights(*keeps[hh], valid)
            pv = jnp.dot(w, v_ref[pl.ds(start, t), lanes[hh]], preferred_element_type=F32)
            acc_ref[hh] = pv if first else acc_ref[hh] + pv

    sweep(qi, causal, True)

    def body(i, _):
        sweep(qi - 1 - i, None, False)
        return 0

    lax.fori_loop(0, qi, body, 0)
    for hh in range(SB_HEADS):
        o_ref[:, hh * HEAD_DIM:(hh + 1) * HEAD_DIM] = acc_ref[hh].astype(o_ref.dtype)


def _sb_prompt(q, k, v, bias, batch, seq):
    t = SB_TILE
    nq = seq // t
    width = SB_HEADS * HEAD_DIM
    q_spec = pl.BlockSpec((t, width), lambda b, hg, qi: (b * nq + qi, hg))
    kv_spec = pl.BlockSpec((seq, width), lambda b, hg, qi: (b, hg))
    return pl.pallas_call(
        _sbp_kernel,
        name="sb_prompt",
        grid=(batch, N_HEADS // SB_HEADS, nq),
        in_specs=[pl.BlockSpec(memory_space=pltpu.SMEM), q_spec, kv_spec, kv_spec],
        out_specs=q_spec,
        out_shape=jax.ShapeDtypeStruct((batch * seq, D_MODEL), BF16),
        scratch_shapes=[
            pltpu.VMEM((SB_HEADS, t, HEAD_DIM), F32),
            pltpu.VMEM((SB_HEADS, t, 1), F32),
        ],
        compiler_params=_params("arbitrary", "arbitrary", "arbitrary"),
    )(bias, q, k, v)


SBS_PAGES = 2
SBS_KEYS = SBS_PAGES * PAGE_SIZE
SBS_GROUPS = N_HEADS // HEADS_PER_TILE


def _sbs_kernel(pt_ref, q_ref, kn_ref, vn_ref, *rest, n_steps, n_tok):
    del pt_ref
    n_parts = SBS_PAGES * SBS_GROUPS
    k_parts, v_parts = rest[:n_parts], rest[n_parts:2 * n_parts]
    bias_ref, o_ref, qbd_ref, acc_ref, car_ref = rest[2 * n_parts:]
    p = pl.program_id(1)
    rows = n_tok * N_HEADS
    tri = _strict_upper2(SBS_KEYS)
    bias = bias_ref[...]

    def process(kcat, vcat, valid):
        carry, acc = _sb_tile(qbd_ref[...], kcat, vcat, bias, tri, car_ref[...], acc_ref[...], valid)
        car_ref[...] = carry
        acc_ref[...] = acc

    @pl.when(p == 0)
    def _():
        lane_head = lax.broadcasted_iota(jnp.int32, (N_HEADS, D_MODEL), 1) // HEAD_DIM
        row_head = lax.broadcasted_iota(jnp.int32, (N_HEADS, D_MODEL), 0)
        own = lane_head == row_head
        for t in range(n_tok):
            qrow = q_ref[t:t + 1, :]
            qbd_ref[t * N_HEADS:(t + 1) * N_HEADS, :] = jnp.where(own, qrow, 0.0).astype(BF16)
        acc_ref[...] = jnp.zeros_like(acc_ref)
        car_ref[...] = jnp.zeros_like(car_ref)
        tok = lax.broadcasted_iota(jnp.int32, (rows, SBS_KEYS), 0) // N_HEADS
        key = lax.broadcasted_iota(jnp.int32, (rows, SBS_KEYS), 1)
        process(kn_ref[...], vn_ref[...], key < tok)

    @pl.when(p > 0)
    def _():
        def page(parts):
            tiles = []
            for ref in parts:
                flat = ref.reshape(PAGE_SIZE * HEADS_PER_TILE, HEAD_DIM)
                for hh in range(HEADS_PER_TILE):
                    tiles.append(flat[pl.ds(hh, PAGE_SIZE, stride=HEADS_PER_TILE), :])
            return jnp.concatenate(tiles, axis=1).astype(BF16)

        def cat(parts):
            pages = [page(parts[g:g + SBS_GROUPS]) for g in range(0, n_parts, SBS_GROUPS)]
            return jnp.concatenate(pages, axis=0)

        process(cat(k_parts), cat(v_parts), None)

    @pl.when(p == n_steps - 1)
    def _():
        row_head = lax.broadcasted_iota(jnp.int32, (rows, HEAD_DIM), 0) % N_HEADS
        out = jnp.zeros((rows, HEAD_DIM), F32)
        for h in range(N_HEADS):
            out = out + jnp.where(row_head == h, acc_ref[:, h * HEAD_DIM:(h + 1) * HEAD_DIM], 0.0)
        o_ref[...] = out


def _sb_sample(q, k_new, v_new, cache_k, cache_v, page_table, bias_col, layer):
    bsz, n_tok, _ = q.shape
    n_pages = page_table.shape[1]
    n_pool = cache_k.shape[0]
    n_layers = cache_k.shape[2]
    rows = n_tok * N_HEADS
    assert n_pages % SBS_PAGES == 0
    n_steps = n_pages // SBS_PAGES + 1
    ck = cache_k.reshape(n_pool, PAGE_SIZE, n_layers, SBS_GROUPS, HEADS_PER_TILE, HEAD_DIM)
    cv = cache_v.reshape(n_pool, PAGE_SIZE, n_layers, SBS_GROUPS, HEADS_PER_TILE, HEAD_DIM)

    def page_spec(age, g):
        def index(b, p, pt):
            first = n_pages - SBS_PAGES * jnp.maximum(p, 1)
            return (pt[b, first + age], 0, layer, g, 0, 0)
        return pl.BlockSpec((None, PAGE_SIZE, None, None, HEADS_PER_TILE, HEAD_DIM), index)

    page_specs = [page_spec(age, g) for age in range(SBS_PAGES) for g in range(SBS_GROUPS)]
    seq_spec = pl.BlockSpec((None, SBS_KEYS, D_MODEL), lambda b, p, pt: (b, 0, 0))
    grid_spec = pltpu.PrefetchScalarGridSpec(
        num_scalar_prefetch=1,
        grid=(bsz, n_steps),
        in_specs=[pl.BlockSpec((None, n_tok, D_MODEL), lambda b, p, pt: (b, 0, 0)), seq_spec, seq_spec]
        + page_specs * 2
        + [pl.BlockSpec((rows, 1), lambda b, p, pt: (0, 0))],
        out_specs=pl.BlockSpec((None, rows, HEAD_DIM), lambda b, p, pt: (b, 0, 0)),
        scratch_shapes=[
            pltpu.VMEM((rows, D_MODEL), BF16),
            pltpu.VMEM((rows, D_MODEL), F32),
            pltpu.VMEM((rows, 1), F32),
        ],
    )
    return pl.pallas_call(
        functools.partial(_sbs_kernel, n_steps=n_steps, n_tok=n_tok),
        name="sb_sample",
        grid_spec=grid_spec,
        out_shape=jax.ShapeDtypeStruct((bsz, rows, HEAD_DIM), F32),
        compiler_params=_params("arbitrary", "arbitrary"),
    )(page_table, q, k_new, v_new, *([ck] * len(page_specs)), *([cv] * len(page_specs)), bias_col)


def _mix_kernel(u_ref, v_ref, g_ref, w_ref, bt_ref, o_ref, *vn_out):
    vn = _rms(v_ref[...], g_ref[...])
    if vn_out:
        vn_out[0][...] = vn
    vnb = vn.astype(BF16)
    r = lax.broadcasted_iota(jnp.int32, (CHUNK, CHUNK), 0)
    c = lax.broadcasted_iota(jnp.int32, (CHUNK, CHUNK), 1)
    lower = c <= r
    for g in range(CM_GROUPS):
        lo, hi = g * CM_GROUP_DIM, (g + 1) * CM_GROUP_DIM
        w = jnp.where(lower, w_ref[g], 0.0).astype(BF16)
        mixed = jnp.dot(w, vnb[:, lo:hi], preferred_element_type=F32) + bt_ref[:, g:g + 1]
        o_ref[:, lo:hi] = (u_ref[:, lo:hi] * mixed).astype(o_ref.dtype)


def _gmlp_mix(z, g_v, w_s, b_s, layer, want_v):
    m = z.shape[0]
    bt = jnp.swapaxes(b_s, 1, 2)
    row = pl.BlockSpec((CHUNK, D_MODEL), lambda c: (c, 0))
    out_specs = [row]
    out_shape = [jax.ShapeDtypeStruct((m, D_MODEL), BF16)]
    if want_v:
        out_specs.append(row)
        out_shape.append(jax.ShapeDtypeStruct((m, D_MODEL), F32))
    return pl.pallas_call(
        _mix_kernel,
        name="gmlp_mix",
        grid=(m // CHUNK,),
        in_specs=[
            row,
            pl.BlockSpec((CHUNK, D_MODEL), lambda c: (c, 1)),
            pl.BlockSpec((None, 1, D_MODEL), lambda c: (layer, 0, 0)),
            pl.BlockSpec((None, CM_GROUPS, CHUNK, CHUNK), lambda c: (layer, 0, 0, 0)),
            pl.BlockSpec((None, CHUNK, CM_GROUPS), lambda c: (layer, 0, 0)),
        ],
        out_specs=out_specs,
        out_shape=out_shape,
        compiler_params=_params("arbitrary"),
    )(z, z, g_v.reshape(g_v.shape[0], 1, D_MODEL), w_s, bt)


def kernel(x_prompt, x_sample, cache_k, cache_v, page_table, p_prompt, p_sample, g_mix_pre, g_mix_post, g_ffn_pre, g_ffn_post, w_qkv, w_o_sb, b_sb, w_uv, g_v, w_s, b_s, w_o_cm, w_up, w_down, w_pl_proj, g_pl, w_pl_gate):
    bp, sp, _ = x_prompt.shape
    bs, ss, _ = x_sample.shape
    mp, msm = bp * sp, bs * ss
    xs = [x_prompt.reshape(mp, D_MODEL), x_sample.reshape(msm, D_MODEL)]
    ps = [p_prompt.reshape(DEPTH, mp, PLE_DIM), p_sample.reshape(DEPTH, msm, PLE_DIM)]
    hs = [_row_call(_norm_kernel, [x], [(g_mix_pre, 0)], [BF16])[0] for x in xs]
    kv_cols = N_SB * D_MODEL
    k_bufs = v_bufs = None
    cm_rows = []

    for i in range(DEPTH):
        j = i // 2
        if i % 2 == 0:
            (q_p,), (q_s,) = _mm(hs[0], hs[1], w_qkv, j, "bf16", n=D_MODEL, out_dtype=BF16)
            (kb_p, k_p), (kb_s, k_s) = _mm(hs[0], hs[1], w_qkv, j, "kv", w_col0=D_MODEL, n=D_MODEL,
                                           kv_bufs=k_bufs, kv_col0=j * D_MODEL, kv_cols=kv_cols)
            (vb_p, v_p), (vb_s, v_s) = _mm(hs[0], hs[1], w_qkv, j, "kv", w_col0=2 * D_MODEL, n=D_MODEL,
                                           kv_bufs=v_bufs, kv_col0=j * D_MODEL, kv_cols=kv_cols)
            k_bufs, v_bufs = (kb_p, kb_s), (vb_p, vb_s)
            att_p = _sb_prompt(q_p, k_p, v_p, b_sb[j], bp, sp)
            pad = ((0, 0), (0, SBS_KEYS - ss), (0, 0))
            k_new = jnp.pad(k_s.reshape(bs, ss, D_MODEL), pad)
            v_new = jnp.pad(v_s.reshape(bs, ss, D_MODEL), pad)
            bias_col = jnp.tile(b_sb[j], ss).reshape(ss * N_HEADS, 1)
            att_s = _sb_sample(q_s.astype(F32).reshape(bs, ss, D_MODEL), k_new, v_new,
                               cache_k, cache_v, page_table, bias_col, j)
            att_s = att_s.reshape(msm, D_MODEL).astype(BF16)
            (mix_p,), (mix_s,) = _mm(att_p, att_s, w_o_sb, j, "f32")
        else:
            (z_p,), (z_s,) = _mm(hs[0], hs[1], w_uv, j, "gelu")
            (gated_p,) = _gmlp_mix(z_p, g_v, w_s, b_s, j, False)
            zp = jnp.pad(z_s.reshape(bs, ss, 2 * D_MODEL), ((0, 0), (0, CHUNK - ss), (0, 0)))
            gated_s, vn = _gmlp_mix(zp.reshape(bs * CHUNK, 2 * D_MODEL), g_v, w_s, b_s, j, True)
            gated_s = gated_s.reshape(bs, CHUNK, D_MODEL)[:, :ss].reshape(msm, D_MODEL)
            cm_rows.append(vn.reshape(bs, CHUNK, D_MODEL)[:, :ss])
            (mix_p,), (mix_s,) = _mm(gated_p, gated_s, w_o_cm, j, "f32")
        x1s, hfs = zip(*[_row_call(_resnorm_kernel, [x, m], [(g_mix_post, i), (g_ffn_pre, i)], [F32, BF16])
                         for x, m in zip(xs, (mix_p, mix_s))])
        (act_p,), (act_s,) = _mm(hfs[0], hfs[1], w_up, i, "relu2", out_dtype=BF16)
        (f_p,), (f_s,) = _mm(act_p, act_s, w_down, i, "f32")
        x2s, hps = zip(*[_row_call(_resnorm_kernel, [x, f], [(g_ffn_post, i), (g_pl, i)], [F32, BF16])
                         for x, f in zip(x1s, (f_p, f_s))])
        (gp_p,), (gp_s,) = _mm(hps[0], hps[1], w_pl_gate, i, "ple", p=ps[0][i], p_s=ps[1][i],
                                wp_stack=w_pl_proj)
        if i + 1 < DEPTH:
            xs, hs = zip(*[_row_call(_addnorm_kernel, [x, g], [(g_mix_pre, i + 1)], [F32, BF16])
                           for x, g in zip(x2s, (gp_p, gp_s))])
        else:
            xs = [_row_call(_add_kernel, [x, g], [], [F32])[0] for x, g in zip(x2s, (gp_p, gp_s))]

    def heads(buf, b, s):
        return buf.reshape(b, s, N_SB, N_HEADS, HEAD_DIM)

    return (
        xs[0].reshape(bp, sp, D_MODEL),
        xs[1].reshape(bs, ss, D_MODEL),
        heads(k_bufs[0], bp, sp),
        heads(v_bufs[0], bp, sp),
        heads(k_bufs[1], bs, ss),
        heads(v_bufs[1], bs, ss),
        jnp.stack(cm_rows, axis=2),
    )
```

```python
import functools

import jax
import jax.numpy as jnp
from jax import lax
from jax.experimental import pallas as pl
from jax.experimental.pallas import tpu as pltpu

D_MODEL = 4096
DEPTH = 4
N_SB = 2
N_HEADS = 32
HEAD_DIM = D_MODEL // N_HEADS
PAGE_SIZE = 128
CHUNK = 128
CM_GROUPS = 16
CM_GROUP_DIM = D_MODEL // CM_GROUPS
PLE_DIM = 256
EPS = 1e-6

V7X_VMEM_LIMIT_BYTES = 56 * 1024 * 1024
HEADS_PER_TILE = 8

MM_TM, MM_TN, MM_TK = 2048, 1024, 1024
MM_CHUNK = 256

F32 = jnp.float32
BF16 = jnp.bfloat16


def _params(*sem):
    return pltpu.CompilerParams(dimension_semantics=sem, vmem_limit_bytes=V7X_VMEM_LIMIT_BYTES)


def _rms(x, g):
    ms = jnp.mean(x * x, axis=-1, keepdims=True)
    return x * lax.rsqrt(ms + EPS) * g


def _norm_kernel(x_ref, g_ref, h_ref):
    h_ref[...] = _rms(x_ref[...], g_ref[...]).astype(BF16)


def _resnorm_kernel(x_ref, m_ref, gp_ref, gn_ref, x1_ref, h_ref):
    x1 = x_ref[...] + _rms(m_ref[...], gp_ref[...])
    x1_ref[...] = x1
    h_ref[...] = _rms(x1, gn_ref[...]).astype(BF16)


def _addnorm_kernel(x_ref, a_ref, gn_ref, x1_ref, h_ref):
    x1 = x_ref[...] + a_ref[...]
    x1_ref[...] = x1
    h_ref[...] = _rms(x1, gn_ref[...]).astype(BF16)


def _add_kernel(x_ref, a_ref, x1_ref):
    x1_ref[...] = x_ref[...] + a_ref[...]


def _row_call(body, arrays, gains, out_dtypes):
    m = arrays[0].shape[0]
    tr = min(m, 256)
    row = pl.BlockSpec((tr, D_MODEL), lambda r: (r, 0))
    in_specs = [row] * len(arrays)
    args = list(arrays)
    for table, layer in gains:
        in_specs.append(pl.BlockSpec((None, 1, D_MODEL), lambda r, layer=layer: (layer, 0, 0)))
        args.append(table.reshape(table.shape[0], 1, D_MODEL))
    return pl.pallas_call(
        body,
        name=body.__name__.strip("_"),
        grid=(m // tr,),
        in_specs=in_specs,
        out_specs=[row] * len(out_dtypes),
        out_shape=[jax.ShapeDtypeStruct((m, D_MODEL), dt) for dt in out_dtypes],
        compiler_params=_params("arbitrary"),
    )(*args)


def _gelu_tanh(x):
    c = 0.7978845608028654
    return 0.5 * x * (1.0 + jnp.tanh(c * (x + 0.044715 * (x * x * x))))


_ACC_IN_OUTPUT = ("f32", "kv", "ple")


def _mm_kernel(*refs, epi, nk, n_alias):
    it = iter(refs)
    a_refs = (next(it), next(it))
    w_ref = next(it)
    p_refs = (None, None)
    wp_ref = None
    if epi == "ple":
        p_refs = (next(it), next(it))
        wp_ref = next(it)
    for _ in range(n_alias):
        next(it)
    outs = []
    for _ in range(2):
        o = next(it)
        ob = next(it) if epi == "kv" else None
        outs.append((o, ob))
    accs = [o for o, _ in outs] if epi in _ACC_IN_OUTPUT else [next(it), next(it)]

    i = pl.program_id(1)
    k = pl.program_id(2)

    def finish(v, o_ref, ob_ref, p_ref):
        if epi == "kv":
            o_ref[...] = v
            ob_ref[...] = v.astype(BF16)
        elif epi == "relu2":
            r = jnp.maximum(v, 0.0)
            o_ref[...] = (r * r).astype(o_ref.dtype)
        elif epi == "gelu":
            o_ref[...] = _gelu_tanh(v).astype(o_ref.dtype)
        elif epi == "ple":
            pp = jnp.dot(p_ref[...].astype(BF16), wp_ref[...].astype(BF16), preferred_element_type=F32)
            o_ref[...] = jax.nn.sigmoid(v) * pp
        else:
            o_ref[...] = v.astype(o_ref.dtype)

    def stream(s, active):
        a_ref, acc_ref = a_refs[s], accs[s]
        o_ref, ob_ref = outs[s]

        def prod():
            a = a_ref[...]
            tn = w_ref.shape[1]
            parts = [jnp.dot(a, w_ref[:, c:c + MM_CHUNK].astype(BF16), preferred_element_type=F32)
                     for c in range(0, tn, MM_CHUNK)]
            return jnp.concatenate(parts, axis=1)

        @pl.when(active & (k == 0))
        def _():
            acc_ref[...] = prod()

        @pl.when(active & (k > 0) & (k < nk - 1))
        def _():
            acc_ref[...] += prod()

        @pl.when(active & (k == nk - 1))
        def _():
            finish(acc_ref[...] + prod(), o_ref, ob_ref, p_refs[s])

    stream(0, True)
    stream(1, i == 0)


def _mm(a, a_s, w_stack, layer, epi, *, w_col0=0, n=None, out_dtype=F32,
        p=None, p_s=None, wp_stack=None, kv_bufs=None, kv_col0=0, kv_cols=None):
    m, kdim = a.shape
    ms = a_s.shape[0]
    n = w_stack.shape[2] if n is None else n
    tm, tn, tk = MM_TM, MM_TN, MM_TK
    nk = kdim // tk
    assert m % tm == 0 and n % tn == 0 and kdim % tk == 0 and nk >= 2
    assert w_col0 % tn == 0 and kv_col0 % tn == 0
    wc, kc = w_col0 // tn, kv_col0 // tn
    grid = (n // tn, m // tm, nk)
    in_specs = [
        pl.BlockSpec((tm, tk), lambda j, i, k: (i, k)),
        pl.BlockSpec((ms, tk), lambda j, i, k: (0, k)),
        pl.BlockSpec((None, tk, tn), lambda j, i, k: (layer, k, wc + j)),
    ]
    args = [a, a_s, w_stack]
    if epi == "ple":
        in_specs += [
            pl.BlockSpec((tm, PLE_DIM), lambda j, i, k: (i, 0)),
            pl.BlockSpec((ms, PLE_DIM), lambda j, i, k: (0, 0)),
            pl.BlockSpec((None, PLE_DIM, tn), lambda j, i, k: (layer, 0, j)),
        ]
        args += [p, p_s, wp_stack]
    aliases = {}
    n_alias = 0
    if kv_bufs is not None:
        n_alias = 2
        for t, buf in enumerate(kv_bufs):
            aliases[len(args)] = 2 * t
            in_specs.append(pl.BlockSpec(memory_space=pl.ANY))
            args.append(buf)
    out_specs, out_shape = [], []
    for rows, tr, row_index in ((m, tm, lambda i: i), (ms, ms, lambda i: 0)):
        if epi == "kv":
            out_specs.append(pl.BlockSpec((tr, tn), lambda j, i, k, f=row_index: (f(i), kc + j)))
            out_shape.append(jax.ShapeDtypeStruct((rows, kv_cols), F32))
            out_specs.append(pl.BlockSpec((tr, tn), lambda j, i, k, f=row_index: (f(i), j)))
            out_shape.append(jax.ShapeDtypeStruct((rows, n), BF16))
        else:
            out_specs.append(pl.BlockSpec((tr, tn), lambda j, i, k, f=row_index: (f(i), j)))
            out_shape.append(jax.ShapeDtypeStruct((rows, n), out_dtype))
    scratch = []
    if epi not in _ACC_IN_OUTPUT:
        scratch = [pltpu.VMEM((tm, tn), F32), pltpu.VMEM((ms, tn), F32)]
    outs = pl.pallas_call(
        functools.partial(_mm_kernel, epi=epi, nk=nk, n_alias=n_alias),
        name=f"mm_{epi}_k{kdim}_n{n}",
        grid=grid,
        in_specs=in_specs,
        out_specs=out_specs,
        out_shape=out_shape,
        scratch_shapes=scratch,
        input_output_aliases=aliases,
        compiler_params=_params("arbitrary", "arbitrary", "arbitrary"),
    )(*args)
    half = len(outs) // 2
    return tuple(outs[:half]), tuple(outs[half:])


def _softplus(z):
    return jnp.maximum(z, 0.0) + jnp.log(1.0 + jnp.exp(-jnp.abs(z)))


def _suffix_sum_lanes(x, tri2):
    hi = x.astype(BF16)
    lo = (x - hi.astype(F32)).astype(BF16)
    return jnp.dot(jnp.concatenate([hi, lo], axis=1), tri2, preferred_element_type=F32)


def _sb_scores(q, kt):
    return lax.dot_general(q, kt, (((1,), (1,)), ((), ())), preferred_element_type=F32)


def _sb_keep(s, bias, tri2, carry, valid):
    z = s * (HEAD_DIM ** -0.5) + bias
    sp = _softplus(z)
    if valid is not None:
        sp = jnp.where(valid, sp, 0.0)
    later = _suffix_sum_lanes(sp, tri2) + carry
    return z - sp, later, later[:, 0:1] + sp[:, 0:1]


def _sb_weights(log_beta, later, valid):
    w = jnp.exp(log_beta - later)
    if valid is not None:
        w = jnp.where(valid, w, 0.0)
    return w.astype(BF16)


def _strict_upper2(n):
    r = lax.broadcasted_iota(jnp.int32, (2 * n, n), 0) % n
    c = lax.broadcasted_iota(jnp.int32, (2 * n, n), 1)
    return (r > c).astype(BF16)


SB_TILE = 256
SB_HEADS = 8


def _sbp_kernel(bias_ref, q_ref, k_ref, v_ref, o_ref, acc_ref, car_ref):
    hg = pl.program_id(1)
    qi = pl.program_id(2)
    t = SB_TILE
    tri = _strict_upper2(t)
    r = lax.broadcasted_iota(jnp.int32, (t, t), 0)
    c = lax.broadcasted_iota(jnp.int32, (t, t), 1)
    causal = c < r

    def sweep(kj, valid, first):
        start = pl.multiple_of(kj * t, t)
        heads = range(SB_HEADS)
        lanes = [slice(hh * HEAD_DIM, (hh + 1) * HEAD_DIM) for hh in heads]
        scores = [_sb_scores(q_ref[:, lanes[hh]], k_ref[pl.ds(start, t), lanes[hh]]) for hh in heads]
        keeps = []
        for hh in heads:
            carry = jnp.zeros((t, 1), F32) if first else car_ref[hh]
            log_beta, later, carry = _sb_keep(scores[hh], bias_ref[hg * SB_HEADS + hh], tri, carry, valid)
            car_ref[hh] = carry
            keeps.append((log_beta, later))
        for hh in heads:
            w = _sb_weights(*keeps[hh], valid)
            pv = jnp.dot(w, v_ref[pl.ds(start, t), lanes[hh]], preferred_element_type=F32)
            acc_ref[hh] = pv if first else acc_ref[hh] + pv

    sweep(qi, causal, True)

    def body(i, _):
        sweep(qi - 1 - i, None, False)
        return 0

    lax.fori_loop(0, qi, body, 0)
    for hh in range(SB_HEADS):
        o_ref[:, hh * HEAD_DIM:(hh + 1) * HEAD_DIM] = acc_ref[hh].astype(o_ref.dtype)


def _sb_prompt(q, k, v, bias, batch, seq):
    t = SB_TILE
    nq = seq // t
    width = SB_HEADS * HEAD_DIM
    q_spec = pl.BlockSpec((t, width), lambda b, hg, qi: (b * nq + qi, hg))
    kv_spec = pl.BlockSpec((seq, width), lambda b, hg, qi: (b, hg))
    return pl.pallas_call(
        _sbp_kernel,
        name="sb_prompt",
        grid=(batch, N_HEADS // SB_HEADS, nq),
        in_specs=[pl.BlockSpec(memory_space=pltpu.SMEM), q_spec, kv_spec, kv_spec],
        out_specs=q_spec,
        out_shape=jax.ShapeDtypeStruct((batch * seq, D_MODEL), BF16),
        scratch_shapes=[
            pltpu.VMEM((SB_HEADS, t, HEAD_DIM), F32),
            pltpu.VMEM((SB_HEADS, t, 1), F32),
        ],
        compiler_params=_params("arbitrary", "arbitrary", "arbitrary"),
    )(bias, q, k, v)


SBS_PAGES = 2
SBS_KEYS = SBS_PAGES * PAGE_SIZE
SBS_GROUPS = N_HEADS // HEADS_PER_TILE


def _sbs_kernel(pt_ref, q_ref, kn_ref, vn_ref, *rest, n_steps, n_tok):
    del pt_ref
    n_parts = SBS_PAGES * SBS_GROUPS
    k_parts, v_parts = rest[:n_parts], rest[n_parts:2 * n_parts]
    bias_ref, o_ref, qbd_ref, acc_ref, car_ref = rest[2 * n_parts:]
    p = pl.program_id(1)
    rows = n_tok * N_HEADS
    tri = _strict_upper2(SBS_KEYS)
    bias = bias_ref[...]

    def process(keys, values, valid):
        scores = _sb_scores(qbd_ref[...], keys())
        vcat = values()
        log_beta, later, carry = _sb_keep(scores, bias, tri, car_ref[...], valid)
        car_ref[...] = carry
        acc_ref[...] += jnp.dot(_sb_weights(log_beta, later, valid), vcat, preferred_element_type=F32)

    @pl.when(p == 0)
    def _():
        lane_head = lax.broadcasted_iota(jnp.int32, (N_HEADS, D_MODEL), 1) // HEAD_DIM
        row_head = lax.broadcasted_iota(jnp.int32, (N_HEADS, D_MODEL), 0)
        own = lane_head == row_head
        for t in range(n_tok):
            qrow = q_ref[t:t + 1, :]
            qbd_ref[t * N_HEADS:(t + 1) * N_HEADS, :] = jnp.where(own, qrow, 0.0).astype(BF16)
        acc_ref[...] = jnp.zeros_like(acc_ref)
        car_ref[...] = jnp.zeros_like(car_ref)
        tok = lax.broadcasted_iota(jnp.int32, (rows, SBS_KEYS), 0) // N_HEADS
        key = lax.broadcasted_iota(jnp.int32, (rows, SBS_KEYS), 1)
        process(lambda: kn_ref[...], lambda: vn_ref[...], key < tok)

    @pl.when(p > 0)
    def _():
        def page(parts):
            tiles = []
            for ref in parts:
                flat = ref.reshape(PAGE_SIZE * HEADS_PER_TILE, HEAD_DIM)
                for hh in range(HEADS_PER_TILE):
                    tiles.append(flat[pl.ds(hh, PAGE_SIZE, stride=HEADS_PER_TILE), :])
            return jnp.concatenate(tiles, axis=1).astype(BF16)

        def cat(parts):
            pages = [page(parts[g:g + SBS_GROUPS]) for g in range(0, n_parts, SBS_GROUPS)]
            return jnp.concatenate(pages, axis=0)

        process(lambda: cat(k_parts), lambda: cat(v_parts), None)

    @pl.when(p == n_steps - 1)
    def _():
        row_head = lax.broadcasted_iota(jnp.int32, (rows, HEAD_DIM), 0) % N_HEADS
        out = jnp.zeros((rows, HEAD_DIM), F32)
        for h in range(N_HEADS):
            out = out + jnp.where(row_head == h, acc_ref[:, h * HEAD_DIM:(h + 1) * HEAD_DIM], 0.0)
        o_ref[...] = out


def _sb_sample(q, k_new, v_new, cache_k, cache_v, page_table, bias_col, layer):
    bsz, n_tok, _ = q.shape
    n_pages = page_table.shape[1]
    n_pool = cache_k.shape[0]
    n_layers = cache_k.shape[2]
    rows = n_tok * N_HEADS
    assert n_pages % SBS_PAGES == 0
    n_steps = n_pages // SBS_PAGES + 1
    ck = cache_k.reshape(n_pool, PAGE_SIZE, n_layers, SBS_GROUPS, HEADS_PER_TILE, HEAD_DIM)
    cv = cache_v.reshape(n_pool, PAGE_SIZE, n_layers, SBS_GROUPS, HEADS_PER_TILE, HEAD_DIM)

    def page_spec(age, g):
        def index(b, p, pt):
            first = n_pages - SBS_PAGES * jnp.maximum(p, 1)
            return (pt[b, first + age], 0, layer, g, 0, 0)
        return pl.BlockSpec((None, PAGE_SIZE, None, None, HEADS_PER_TILE, HEAD_DIM), index)

    page_specs = [page_spec(age, g) for age in range(SBS_PAGES) for g in range(SBS_GROUPS)]
    seq_spec = pl.BlockSpec((None, SBS_KEYS, D_MODEL), lambda b, p, pt: (b, 0, 0))
    grid_spec = pltpu.PrefetchScalarGridSpec(
        num_scalar_prefetch=1,
        grid=(bsz, n_steps),
        in_specs=[pl.BlockSpec((None, n_tok, D_MODEL), lambda b, p, pt: (b, 0, 0)), seq_spec, seq_spec]
        + page_specs * 2
        + [pl.BlockSpec((rows, 1), lambda b, p, pt: (0, 0))],
        out_specs=pl.BlockSpec((None, rows, HEAD_DIM), lambda b, p, pt: (b, 0, 0)),
        scratch_shapes=[
            pltpu.VMEM((rows, D_MODEL), BF16),
            pltpu.VMEM((rows, D_MODEL), F32),
            pltpu.VMEM((rows, 1), F32),
        ],
    )
    return pl.pallas_call(
        functools.partial(_sbs_kernel, n_steps=n_steps, n_tok=n_tok),
        name="sb_sample",
        grid_spec=grid_spec,
        out_shape=jax.ShapeDtypeStruct((bsz, rows, HEAD_DIM), F32),
        compiler_params=_params("arbitrary", "arbitrary"),
    )(page_table, q, k_new, v_new, *([ck] * len(page_specs)), *([cv] * len(page_specs)), bias_col)


def _mix_kernel(u_ref, v_ref, g_ref, w_ref, bt_ref, o_ref, *vn_out):
    vn = _rms(v_ref[...], g_ref[...])
    if vn_out:
        vn_out[0][...] = vn
    vnb = vn.astype(BF16)
    r = lax.broadcasted_iota(jnp.int32, (CHUNK, CHUNK), 0)
    c = lax.broadcasted_iota(jnp.int32, (CHUNK, CHUNK), 1)
    lower = c <= r
    for g in range(CM_GROUPS):
        lo, hi = g * CM_GROUP_DIM, (g + 1) * CM_GROUP_DIM
        w = jnp.where(lower, w_ref[g], 0.0).astype(BF16)
        mixed = jnp.dot(w, vnb[:, lo:hi], preferred_element_type=F32) + bt_ref[:, g:g + 1]
        o_ref[:, lo:hi] = (u_ref[:, lo:hi] * mixed).astype(o_ref.dtype)


def _gmlp_mix(z, g_v, w_s, b_s, layer, want_v):
    m = z.shape[0]
    bt = jnp.swapaxes(b_s, 1, 2)
    row = pl.BlockSpec((CHUNK, D_MODEL), lambda c: (c, 0))
    out_specs = [row]
    out_shape = [jax.ShapeDtypeStruct((m, D_MODEL), BF16)]
    if want_v:
        out_specs.append(row)
        out_shape.append(jax.ShapeDtypeStruct((m, D_MODEL), F32))
    return pl.pallas_call(
        _mix_kernel,
        name="gmlp_mix",
        grid=(m // CHUNK,),
        in_specs=[
            row,
            pl.BlockSpec((CHUNK, D_MODEL), lambda c: (c, 1)),
            pl.BlockSpec((None, 1, D_MODEL), lambda c: (layer, 0, 0)),
            pl.BlockSpec((None, CM_GROUPS, CHUNK, CHUNK), lambda c: (layer, 0, 0, 0)),
            pl.BlockSpec((None, CHUNK, CM_GROUPS), lambda c: (layer, 0, 0)),
        ],
        out_specs=out_specs,
        out_shape=out_shape,
        compiler_params=_params("arbitrary"),
    )(z, z, g_v.reshape(g_v.shape[0], 1, D_MODEL), w_s, bt)


def kernel(x_prompt, x_sample, cache_k, cache_v, page_table, p_prompt, p_sample, g_mix_pre, g_mix_post, g_ffn_pre, g_ffn_post, w_qkv, w_o_sb, b_sb, w_uv, g_v, w_s, b_s, w_o_cm, w_up, w_down, w_pl_proj, g_pl, w_pl_gate):
    bp, sp, _ = x_prompt.shape
    bs, ss, _ = x_sample.shape
    mp, msm = bp * sp, bs * ss
    xs = [x_prompt.reshape(mp, D_MODEL), x_sample.reshape(msm, D_MODEL)]
    ps = [p_prompt.reshape(DEPTH, mp, PLE_DIM), p_sample.reshape(DEPTH, msm, PLE_DIM)]
    hs = [_row_call(_norm_kernel, [x], [(g_mix_pre, 0)], [BF16])[0] for x in xs]
    kv_cols = N_SB * D_MODEL
    k_bufs = (jnp.zeros((mp, kv_cols), F32), jnp.zeros((msm, kv_cols), F32))
    v_bufs = (jnp.zeros((mp, kv_cols), F32), jnp.zeros((msm, kv_cols), F32))
    cm_rows = []

    for i in range(DEPTH):
        j = i // 2
        if i % 2 == 0:
            (q_p,), (q_s,) = _mm(hs[0], hs[1], w_qkv, j, "bf16", n=D_MODEL, out_dtype=BF16)
            (kb_p, k_p), (kb_s, k_s) = _mm(hs[0], hs[1], w_qkv, j, "kv", w_col0=D_MODEL, n=D_MODEL,
                                           kv_bufs=k_bufs, kv_col0=j * D_MODEL, kv_cols=kv_cols)
            (vb_p, v_p), (vb_s, v_s) = _mm(hs[0], hs[1], w_qkv, j, "kv", w_col0=2 * D_MODEL, n=D_MODEL,
                                           kv_bufs=v_bufs, kv_col0=j * D_MODEL, kv_cols=kv_cols)
            k_bufs, v_bufs = (kb_p, kb_s), (vb_p, vb_s)
            att_p = _sb_prompt(q_p, k_p, v_p, b_sb[j], bp, sp)
            pad = ((0, 0), (0, SBS_KEYS - ss), (0, 0))
            k_new = jnp.pad(k_s.reshape(bs, ss, D_MODEL), pad)
            v_new = jnp.pad(v_s.reshape(bs, ss, D_MODEL), pad)
            bias_col = jnp.tile(b_sb[j], ss).reshape(ss * N_HEADS, 1)
            att_s = _sb_sample(q_s.astype(F32).reshape(bs, ss, D_MODEL), k_new, v_new,
                               cache_k, cache_v, page_table, bias_col, j)
            att_s = att_s.reshape(msm, D_MODEL).astype(BF16)
            (mix_p,), (mix_s,) = _mm(att_p, att_s, w_o_sb, j, "f32")
        else:
            (z_p,), (z_s,) = _mm(hs[0], hs[1], w_uv, j, "gelu")
            (gated_p,) = _gmlp_mix(z_p, g_v, w_s, b_s, j, False)
            zp = jnp.pad(z_s.reshape(bs, ss, 2 * D_MODEL), ((0, 0), (0, CHUNK - ss), (0, 0)))
            gated_s, vn = _gmlp_mix(zp.reshape(bs * CHUNK, 2 * D_MODEL), g_v, w_s, b_s, j, True)
            gated_s = gated_s.reshape(bs, CHUNK, D_MODEL)[:, :ss].reshape(msm, D_MODEL)
            cm_rows.append(vn.reshape(bs, CHUNK, D_MODEL)[:, :ss])
            (mix_p,), (mix_s,) = _mm(gated_p, gated_s, w_o_cm, j, "f32")
        x1s, hfs = zip(*[_row_call(_resnorm_kernel, [x, m], [(g_mix_post, i), (g_ffn_pre, i)], [F32, BF16])
                         for x, m in zip(xs, (mix_p, mix_s))])
        (act_p,), (act_s,) = _mm(hfs[0], hfs[1], w_up, i, "relu2", out_dtype=BF16)
        (f_p,), (f_s,) = _mm(act_p, act_s, w_down, i, "f32")
        x2s, hps = zip(*[_row_call(_resnorm_kernel, [x, f], [(g_ffn_post, i), (g_pl, i)], [F32, BF16])
                         for x, f in zip(x1s, (f_p, f_s))])
        (gp_p,), (gp_s,) = _mm(hps[0], hps[1], w_pl_gate, i, "ple", p=ps[0][i], p_s=ps[1][i],
                                wp_stack=w_pl_proj)
        if i + 1 < DEPTH:
            xs, hs = zip(*[_row_call(_addnorm_kernel, [x, g], [(g_mix_pre, i + 1)], [F32, BF16])
                           for x, g in zip(x2s, (gp_p, gp_s))])
        else:
            xs = [_row_call(_add_kernel, [x, g], [], [F32])[0] for x, g in zip(x2s, (gp_p, gp_s))]

    def heads(buf, b, s):
        return buf.reshape(b, s, N_SB, N_HEADS, HEAD_DIM)

    return (
        xs[0].reshape(bp, sp, D_MODEL),
        xs[1].reshape(bs, ss, D_MODEL),
        heads(k_bufs[0], bp, sp),
        heads(v_bufs[0], bp, sp),
        heads(k_bufs[1], bs, ss),
        heads(v_bufs[1], bs, ss),
        jnp.stack(cm_rows, axis=2),
    )
```
